```python
import math
import jax
import jax.numpy as jnp
from jax import lax
import numpy as np

D_MODEL = 1024
BATCH = 8
SEQ = 2048
DEPTH = 2

CTX_LEN = 256
GRID_W = 64

HG_HEADS = 4
HG_DK = 128
HG_DV = 128
HG_K = HG_HEADS * HG_DK
HG_V = HG_HEADS * HG_DV
HG_CHUNK = 64

RW_HEADS = 8
RW_HD = 64
RW_C = RW_HEADS * RW_HD
RW_W_RANK = 64
RW_A_RANK = 64
RW_G_RANK = 128
RW_LN_EPS = 64e-5

DA_HEADS = 4
DA_HD = 64
DA_QK = DA_HEADS * 2 * DA_HD
DA_V = DA_HEADS * 2 * DA_HD
DA_Q_BLOCK = 128
DA_SUBLN_EPS = 1e-5
DA_SCALE = DA_HD ** -0.5
ROPE_AXIS_DIM = DA_HD // 2
ROPE_BASE = 10000.0

FFN_DIM = 2816
N_EXPERTS = 8
TOP_K = 2
N_DENSE = (DEPTH + 1) // 2
N_MOE = DEPTH // 2
NORM_EPS = 1e-6

HG_SPLITS = (('q', HG_K), ('f_fwd', HG_K), ('f_bwd', HG_K), ('i', HG_V), ('g', HG_V))
RW_SPLITS = (('r', RW_C), ('k', RW_C), ('v', RW_C), ('wd_fwd', RW_W_RANK), ('wd_bwd', RW_W_RANK), ('ad', RW_A_RANK), ('gd', RW_G_RANK))
DA_SPLITS = (('q', DA_QK), ('k', DA_QK), ('v', DA_V))
GATE_SPLITS = (('hg', D_MODEL), ('rw', D_MODEL), ('da', D_MODEL))
HG_COLS = sum(w for _, w in HG_SPLITS)
RW_COLS = sum(w for _, w in RW_SPLITS)
DA_COLS = sum(w for _, w in DA_SPLITS)
GATE_COLS = sum(w for _, w in GATE_SPLITS)
IN_COLS = HG_COLS + RW_COLS + DA_COLS + GATE_COLS
GROUP_BOUNDS = (HG_COLS, HG_COLS + RW_COLS, HG_COLS + RW_COLS + DA_COLS)

kernel_name = 'hybrid_flow_hgrn2_rwkv7_diffattn_moe'


def rms_norm(x, g, eps=NORM_EPS):
    xf = x.astype(jnp.float32)
    y = xf * lax.rsqrt(jnp.mean(xf * xf, axis=-1, keepdims=True) + eps)
    return (y * g).astype(x.dtype)


def modulate(h, shift, scale):
    return h * (1.0 + scale) + shift


def split_named(p, splits):
    bounds = np.cumsum([w for _, w in splits])[:-1].tolist()
    return dict(zip([n for n, _ in splits], jnp.split(p, bounds, axis=-1)))


def to_heads(a, n_heads):
    bsz, t, _ = a.shape
    return a.reshape(bsz, t, n_heads, -1).transpose(0, 2, 1, 3)


def hgrn2_lower_bounds(lb_logits):
    p = jax.nn.softmax(lb_logits.astype(jnp.float32), axis=0)
    cs = jnp.cumsum(p, axis=0)
    return cs - cs[:1]


def hgrn2_log_forget(z, lb):
    z = z.astype(jnp.float32)
    return jnp.logaddexp(jnp.log(lb), jnp.log1p(-lb) + jax.nn.log_sigmoid(z))


def gla_chunk_scan(q, k, v, logf, s0, need_out):
    bsz, nh, t, _ = q.shape
    n = t // HG_CHUNK

    def chunks(a):
        return a.reshape(bsz, nh, n, HG_CHUNK, a.shape[-1]).transpose(2, 0, 1, 3, 4)

    incl = jnp.tril(jnp.ones((HG_CHUNK, HG_CHUNK), bool))[:, :, None]

    def step(state, inp):
        qc, kc, vc, lc = inp
        b = jnp.cumsum(lc, axis=2)
        b_end = b[:, :, -1:, :]
        new_state = (jnp.swapaxes(jnp.exp(b_end), 2, 3) * state
                     + jnp.einsum('bhsk,bhsv->bhkv', kc * jnp.exp(b_end - b), vc))
        if not need_out:
            return new_state, None
        o_inter = jnp.einsum('bhtk,bhkv->bhtv', qc * jnp.exp(b), state)
        decay = jnp.exp(jnp.where(incl, b[:, :, :, None, :] - b[:, :, None, :, :], -jnp.inf))
        attn = jnp.einsum('bhtk,bhsk,bhtsk->bhts', qc, kc, decay)
        return new_state, o_inter + jnp.einsum('bhts,bhsv->bhtv', attn, vc)

    state, o = lax.scan(step, s0, (chunks(q), chunks(k), chunks(v), chunks(logf)))
    if need_out:
        o = o.transpose(1, 2, 0, 3, 4).reshape(bsz, nh, t, -1)
    return state, o


def hgrn2_branch(p_lat, p_ctx, lb, norm_g, proj, need_ctx):
    streams = []
    for p in (p_lat, p_ctx):
        pd = split_named(p, HG_SPLITS)
        q = to_heads(jax.nn.silu(pd['q'].astype(jnp.float32)), HG_HEADS)
        v = to_heads(pd['i'].astype(jnp.float32), HG_HEADS)
        logfs = [to_heads(hgrn2_log_forget(pd[nm], lb[d]), HG_HEADS) for d, nm in enumerate(('f_fwd', 'f_bwd'))]
        streams.append((pd, q, v, logfs))
    (pd_l, q_l, v_l, lf_l), (pd_c, q_c, v_c, lf_c) = streams
    bsz = q_l.shape[0]
    s0 = jnp.zeros((bsz, HG_HEADS, HG_DK, HG_DV), jnp.float32)
    o_lat, o_ctx = 0.0, 0.0
    for d in range(2):
        flip = (lambda a: jnp.flip(a, axis=2)) if d == 1 else (lambda a: a)
        s_ctx, oc = gla_chunk_scan(flip(q_c), flip(-jnp.expm1(lf_c[d])), flip(v_c), flip(lf_c[d]), s0, need_ctx)
        _, ol = gla_chunk_scan(flip(q_l), flip(-jnp.expm1(lf_l[d])), flip(v_l), flip(lf_l[d]), s_ctx, True)
        o_lat = o_lat + flip(ol)
        if need_ctx:
            o_ctx = o_ctx + flip(oc)

    def finish(o, pd):
        b_, _, t, _ = o.shape
        o = rms_norm(o, norm_g).transpose(0, 2, 1, 3).reshape(b_, t, HG_V)
        return (o * jax.nn.silu(pd['g'].astype(jnp.float32))) @ proj

    return finish(o_lat, pd_l), (finish(o_ctx, pd_c) if need_ctx else None)


def token_shift_mix(p, mu):
    pad = jnp.pad(p, ((0, 0), (1, 1), (0, 0)))
    nb = 0.5 * (pad[:, :-2] + pad[:, 2:])
    return p + mu * (nb - p)


def rwkv7_prepare(p, mu, w0, w2, a0, a2, g2, k_k, k_a):
    pd = split_named(token_shift_mix(p, mu).astype(jnp.float32), RW_SPLITS)
    bsz, t, _ = p.shape
    rh = lambda z: z.reshape(bsz, t, RW_HEADS, RW_HD)
    a = jax.nn.sigmoid(a0 + pd['ad'] @ a2)
    kk = rh(pd['k'] * k_k)
    kk = kk / jnp.maximum(jnp.sqrt(jnp.sum(kk * kk, axis=-1, keepdims=True)), 1e-12)
    k = pd['k'] * (1.0 + (a - 1.0) * k_a)
    decays = []
    for d, nm in enumerate(('wd_fwd', 'wd_bwd')):
        wlog = -jax.nn.softplus(-(w0[d] + jnp.tanh(pd[nm]) @ w2[d])) - 0.5
        decays.append(rh(jnp.exp(-jnp.exp(wlog))))
    return dict(r=rh(pd['r']), k=rh(k), v=rh(pd['v']), kk=kk, a=rh(a),
                g=jax.nn.sigmoid(pd['gd']) @ g2, decays=decays)


def rwkv7_scan(r, w, k, v, a, b, s0, need_out, reverse):
    tm = lambda z: jnp.moveaxis(z, 1, 0)

    def step(state, inp):
        rt, wt, kt, vt, at, bt = inp
        sa = jnp.einsum('bhvk,bhk->bhv', state, at)
        state = state * wt[:, :, None, :] + sa[..., None] * bt[:, :, None, :] + vt[..., None] * kt[:, :, None, :]
        if not need_out:
            return state, None
        return state, jnp.einsum('bhvk,bhk->bhv', state, rt)

    state, y = lax.scan(step, s0, (tm(r), tm(w), tm(k), tm(v), tm(a), tm(b)), reverse=reverse)
    return state, (jnp.moveaxis(y, 0, 1) if need_out else None)


def rwkv7_finish(y, st, r_k, ln_g, ln_b, proj):
    bsz, t = y.shape[:2]
    mean = jnp.mean(y, axis=-1, keepdims=True)
    var = jnp.mean(jnp.square(y - mean), axis=-1, keepdims=True)
    yn = ((y - mean) * lax.rsqrt(var + RW_LN_EPS)).reshape(bsz, t, RW_C) * ln_g + ln_b
    bonus = jnp.sum(st['r'] * st['k'] * r_k, axis=-1, keepdims=True) * st['v']
    return ((yn + bonus.reshape(bsz, t, RW_C)) * st['g']) @ proj


def rwkv7_branch(p_lat, p_ctx, mu, w0, w2, a0, a2, g2, k_k, k_a, r_k, ln_g, ln_b, proj, need_ctx):
    sl = rwkv7_prepare(p_lat, mu, w0, w2, a0, a2, g2, k_k, k_a)
    sc = rwkv7_prepare(p_ctx, mu, w0, w2, a0, a2, g2, k_k, k_a)
    bsz = p_lat.shape[0]
    s0 = jnp.zeros((bsz, RW_HEADS, RW_HD, RW_HD), jnp.float32)
    y_lat, y_ctx = 0.0, 0.0
    for d in range(2):
        rev = d == 1
        s_ctx, yc = rwkv7_scan(sc['r'], sc['decays'][d], sc['k'], sc['v'], -sc['kk'], sc['kk'] * sc['a'], s0, need_ctx, rev)
        _, yl = rwkv7_scan(sl['r'], sl['decays'][d], sl['k'], sl['v'], -sl['kk'], sl['kk'] * sl['a'], s_ctx, True, rev)
        y_lat = y_lat + yl
        if need_ctx:
            y_ctx = y_ctx + yc
    out_l = rwkv7_finish(y_lat, sl, r_k, ln_g, ln_b, proj)
    out_c = rwkv7_finish(y_ctx, sc, r_k, ln_g, ln_b, proj) if need_ctx else None
    return out_l, out_c


def axial_rope_tables(rows):
    row = jnp.repeat(jnp.arange(rows), GRID_W).astype(jnp.float32)
    col = jnp.tile(jnp.arange(GRID_W), rows).astype(jnp.float32)
    inv_freq = 1.0 / (ROPE_BASE ** (jnp.arange(0, ROPE_AXIS_DIM, 2, dtype=jnp.float32) / ROPE_AXIS_DIM))
    ang_r = row[:, None] * inv_freq
    ang_c = col[:, None] * inv_freq
    return (jnp.cos(ang_r), jnp.sin(ang_r), jnp.cos(ang_c), jnp.sin(ang_c))


def rotate_pairs(x, cos, sin):
    x1, x2 = jnp.split(x, 2, axis=-1)
    return jnp.concatenate([x1 * cos - x2 * sin, x2 * cos + x1 * sin], axis=-1)


def apply_axial_rope(x, tables):
    cr, sr, cc, sc = [tb[:, None, None, :] for tb in tables]
    xr, xc = jnp.split(x.astype(jnp.float32), 2, axis=-1)
    return jnp.concatenate([rotate_pairs(xr, cr, sr), rotate_pairs(xc, cc, sc)], axis=-1).astype(x.dtype)


def diff_softmax_attend(q, k, v, lam):
    s = jnp.einsum('bqhmd,bkhmd->bhmqk', q, k).astype(jnp.float32) * DA_SCALE
    p = jax.nn.softmax(s, axis=-1)
    w = p[:, :, 0] - lam * p[:, :, 1]
    return jnp.einsum('bhqk,bkhe->bqhe', w.astype(v.dtype), v)


def diff_attn_branch(p_lat, p_ctx, lam_p, subln_g, proj, layer, rope, need_ctx):
    lat = split_named(p_lat, DA_SPLITS)
    cx = split_named(p_ctx, DA_SPLITS)
    bsz, seq, _ = p_lat.shape
    qk = lambda a: a.reshape(a.shape[0], a.shape[1], DA_HEADS, 2, DA_HD)
    vv = lambda a: a.reshape(a.shape[0], a.shape[1], DA_HEADS, 2 * DA_HD)
    q_l = apply_axial_rope(qk(lat['q']), rope)
    k_l = apply_axial_rope(qk(lat['k']), rope)
    k_c, v_c = qk(cx['k']), vv(cx['v'])
    lam_init = 0.8 - 0.6 * math.exp(-0.3 * layer)
    lp = lam_p.astype(jnp.float32)
    lam = jnp.exp(jnp.sum(lp[0] * lp[1])) - jnp.exp(jnp.sum(lp[2] * lp[3])) + lam_init
    k_all = jnp.concatenate([k_c, k_l], axis=1)
    v_all = jnp.concatenate([v_c, vv(lat['v'])], axis=1)
    nb = seq // DA_Q_BLOCK
    qb = q_l.reshape(bsz, nb, DA_Q_BLOCK, DA_HEADS, 2, DA_HD).swapaxes(0, 1)
    ob = lax.map(lambda qq: diff_softmax_attend(qq, k_all, v_all, lam), qb)
    o_l = ob.swapaxes(0, 1).reshape(bsz, seq, DA_HEADS, 2 * DA_HD)

    def finish(o):
        o = rms_norm(o, subln_g, DA_SUBLN_EPS) * (1.0 - lam_init)
        return o.reshape(o.shape[0], o.shape[1], DA_V) @ proj

    y_c = finish(diff_softmax_attend(qk(cx['q']), k_c, v_c, lam)) if need_ctx else None
    return finish(o_l), y_c


def gated_merge(p_gate, y_hg, y_rw, y_da, w_out):
    g = split_named(p_gate, GATE_SPLITS)
    m = jax.nn.sigmoid(g['hg']) * y_hg + jax.nn.sigmoid(g['rw']) * y_rw + jax.nn.sigmoid(g['da']) * y_da
    return m @ w_out


def swiglu(h, w1, w3, w2):
    return (jax.nn.silu(h @ w1) * (h @ w3)) @ w2


def moe_swiglu(h, router, w1, w3, w2):
    shp = h.shape
    t = h.reshape(-1, shp[-1])
    logits = (t @ router).astype(jnp.float32)
    top_v, top_i = lax.top_k(logits, TOP_K)
    wts = jax.nn.softmax(top_v, axis=-1)
    combine = jnp.sum(jax.nn.one_hot(top_i, N_EXPERTS, dtype=jnp.float32) * wts[..., None], axis=1)
    out = jnp.zeros(t.shape, jnp.float32)
    for e in range(N_EXPERTS):
        out = out + combine[:, e:e + 1] * swiglu(t, w1[e], w3[e], w2[e])
    return out.astype(h.dtype).reshape(shp)


def channel_mix(h, layer, ffn_w1, ffn_w3, ffn_w2, moe_router, moe_w1, moe_w3, moe_w2):
    j = layer // 2
    if layer % 2 == 0:
        return swiglu(h, ffn_w1[j], ffn_w3[j], ffn_w2[j])
    return moe_swiglu(h, moe_router[j], moe_w1[j], moe_w3[j], moe_w2[j])


def setup_inputs(seed: int = 0) -> dict:
    key = jax.random.key(seed)
    keys = list(jax.random.split(key, 48))

    def nrm(shape, scale):
        return jax.random.normal(keys.pop(), shape, jnp.float32) * scale

    def unif(shape, lo, hi):
        return jax.random.uniform(keys.pop(), shape, jnp.float32, lo, hi)

    def gain(shape):
        return 1.0 + nrm(shape, 0.02)

    D = D_MODEL
    return {
        'x': nrm((BATCH, SEQ, D), 1.0),
        'c': nrm((BATCH, D), 1.0),
        'ctx': nrm((BATCH, CTX_LEN, D), 1.0),
        'c_ctx': nrm((D,), 1.0),
        'ada_w': nrm((DEPTH, D, 6 * D), 0.5 * D ** -0.5),
        'ada_b': nrm((DEPTH, 6 * D), 0.01),
        'norm_mix_g': gain((DEPTH, D)),
        'norm_ffn_g': gain((DEPTH, D)),
        'final_norm_g': gain((D,)),
        'w_in': nrm((DEPTH, D, IN_COLS), D ** -0.5),
        'hg_lb_logits': nrm((DEPTH, 2, HG_K), 1.0),
        'hg_norm_g': gain((DEPTH, HG_DV)),
        'hg_proj': nrm((DEPTH, HG_V, D), HG_V ** -0.5),
        'rw_mu': unif((DEPTH, RW_COLS), 0.0, 1.0),
        'rw_w0': unif((DEPTH, 2, RW_C), -6.0, -1.0),
        'rw_w2': nrm((DEPTH, 2, RW_W_RANK, RW_C), 0.1 * RW_W_RANK ** -0.5),
        'rw_a0': nrm((DEPTH, RW_C), 0.5),
        'rw_a2': nrm((DEPTH, RW_A_RANK, RW_C), 0.5 * RW_A_RANK ** -0.5),
        'rw_g2': nrm((DEPTH, RW_G_RANK, RW_C), RW_G_RANK ** -0.5),
        'rw_k_k': 0.85 + nrm((DEPTH, RW_C), 0.05),
        'rw_k_a': 1.0 + nrm((DEPTH, RW_C), 0.05),
        'rw_r_k': nrm((DEPTH, RW_HEADS, RW_HD), 0.1),
        'rw_ln_g': gain((DEPTH, RW_C)),
        'rw_ln_b': nrm((DEPTH, RW_C), 0.01),
        'rw_proj': nrm((DEPTH, RW_C, D), RW_C ** -0.5),
        'da_lambda': nrm((DEPTH, 4, DA_HD), 0.1),
        'da_subln_g': gain((DEPTH, 2 * DA_HD)),
        'da_proj': nrm((DEPTH, DA_V, D), DA_V ** -0.5),
        'w_out': nrm((DEPTH, D, D), D ** -0.5),
        'ffn_w1': nrm((N_DENSE, D, FFN_DIM), D ** -0.5),
        'ffn_w3': nrm((N_DENSE, D, FFN_DIM), D ** -0.5),
        'ffn_w2': nrm((N_DENSE, FFN_DIM, D), FFN_DIM ** -0.5),
        'moe_router': nrm((N_MOE, D, N_EXPERTS), D ** -0.5),
        'moe_w1': nrm((N_MOE, N_EXPERTS, D, FFN_DIM), D ** -0.5),
        'moe_w3': nrm((N_MOE, N_EXPERTS, D, FFN_DIM), D ** -0.5),
        'moe_w2': nrm((N_MOE, N_EXPERTS, FFN_DIM, D), FFN_DIM ** -0.5),
    }


def reference(x, c, ctx, c_ctx, ada_w, ada_b, norm_mix_g, norm_ffn_g, final_norm_g, w_in,
              hg_lb_logits, hg_norm_g, hg_proj,
              rw_mu, rw_w0, rw_w2, rw_a0, rw_a2, rw_g2, rw_k_k, rw_k_a, rw_r_k, rw_ln_g, rw_ln_b, rw_proj,
              da_lambda, da_subln_g, da_proj, w_out,
              ffn_w1, ffn_w3, ffn_w2, moe_router, moe_w1, moe_w3, moe_w2):
    seq = x.shape[1]
    rows = seq // GRID_W
    rope = axial_rope_tables(rows)
    lower_bounds = hgrn2_lower_bounds(hg_lb_logits)
    sc_lat = jax.nn.silu(c)
    sc_ctx = jax.nn.silu(c_ctx)
    xc = ctx
    for l in range(DEPTH):
        need_ctx = l < DEPTH - 1
        ml = [m[:, None, :] for m in jnp.split(sc_lat @ ada_w[l] + ada_b[l], 6, axis=-1)]
        mc = jnp.split(sc_ctx @ ada_w[l] + ada_b[l], 6, axis=-1)
        pl = modulate(rms_norm(x, norm_mix_g[l]), ml[0], ml[1]) @ w_in[l]
        pc = modulate(rms_norm(xc, norm_mix_g[l]), mc[0], mc[1]) @ w_in[l]
        hg_l, rw_l, da_l, gt_l = jnp.split(pl, GROUP_BOUNDS, axis=-1)
        hg_c, rw_c, da_c, gt_c = jnp.split(pc, GROUP_BOUNDS, axis=-1)
        ya_l, ya_c = hgrn2_branch(hg_l, hg_c, lower_bounds[l], hg_norm_g[l], hg_proj[l], need_ctx)
        yb_l, yb_c = rwkv7_branch(rw_l, rw_c, rw_mu[l], rw_w0[l], rw_w2[l], rw_a0[l], rw_a2[l], rw_g2[l],
                                  rw_k_k[l], rw_k_a[l], rw_r_k[l], rw_ln_g[l], rw_ln_b[l], rw_proj[l], need_ctx)
        yc_l, yc_c = diff_attn_branch(da_l, da_c, da_lambda[l], da_subln_g[l], da_proj[l], l, rope, need_ctx)
        x = x + (ml[2] * gated_merge(gt_l, ya_l, yb_l, yc_l, w_out[l])).astype(x.dtype)
        if need_ctx:
            xc = xc + (mc[2] * gated_merge(gt_c, ya_c, yb_c, yc_c, w_out[l])).astype(xc.dtype)
        hl = modulate(rms_norm(x, norm_ffn_g[l]), ml[3], ml[4])
        x = x + (ml[5] * channel_mix(hl, l, ffn_w1, ffn_w3, ffn_w2, moe_router, moe_w1, moe_w3, moe_w2)).astype(x.dtype)
        if need_ctx:
            hc = modulate(rms_norm(xc, norm_ffn_g[l]), mc[3], mc[4])
            xc = xc + (mc[5] * channel_mix(hc, l, ffn_w1, ffn_w3, ffn_w2, moe_router, moe_w1, moe_w3, moe_w2)).astype(xc.dtype)
    return rms_norm(x, final_norm_g)
```

```python
import functools
import math

import numpy as np
import jax
import jax.numpy as jnp
from jax import lax
from jax.experimental import pallas as pl
from jax.experimental.pallas import tpu as pltpu

F32 = jnp.float32
BF16 = jnp.bfloat16

GRID_W = 64
HG_HEADS = 4
HG_DK = 128
RW_HEADS = 8
RW_HD = 64
RW_C = RW_HEADS * RW_HD
RW_W_RANK = 64
RW_A_RANK = 64
RW_G_RANK = 128
RW_LN_EPS = 64e-5
DA_HEADS = 4
DA_HD = 64
DA_SUBLN_EPS = 1e-5
DA_SCALE = DA_HD ** -0.5
ROPE_AXIS_DIM = DA_HD // 2
ROPE_BASE = 10000.0
N_EXPERTS = 8
NORM_EPS = 1e-6

LANES = 128
GLA_BLOCK = 16
PREP_TILE = 256
SCAN_BLOCK = 128
VMEM_LIMIT = 52 * 1024 * 1024


def _cparams(*sem):
    return pltpu.CompilerParams(dimension_semantics=sem, vmem_limit_bytes=VMEM_LIMIT)


def _dot(a, b):
    return jnp.dot(a, b, preferred_element_type=F32)


def _dot_nt(a, b):
    return lax.dot_general(a, b, (((1,), (1,)), ((), ())), preferred_element_type=F32)


def _dot_tn(a, b):
    return lax.dot_general(a, b, (((0,), (0,)), ((), ())), preferred_element_type=F32)


def _split2(x):
    hi = x.astype(BF16)
    lo = (x - hi.astype(F32)).astype(BF16)
    return hi, lo


def _split3(x):
    hi = x.astype(BF16)
    r = x - hi.astype(F32)
    mid = r.astype(BF16)
    lo = (r - mid.astype(F32)).astype(BF16)
    return hi, mid, lo


def _dot_hi(a, b):
    ah, al = _split2(a)
    bh, bl = _split2(b)
    return _dot(ah, bh) + (_dot(ah, bl) + _dot(al, bh))


def _dot_x3(x, m):
    hi, mid, lo = _split3(x)
    return _dot(hi, m) + (_dot(mid, m) + _dot(lo, m))


def _dot_m3(m, x):
    hi, mid, lo = _split3(x)
    return _dot(m, hi) + (_dot(m, mid) + _dot(m, lo))


def _sigmoid(x):
    return jax.nn.sigmoid(x)


def _silu(x):
    return x * jax.nn.sigmoid(x)


def _log_sigmoid(z):
    return jnp.minimum(z, 0.0) - jnp.log1p(jnp.exp(-jnp.abs(z)))


def _softplus(x):
    return jnp.maximum(x, 0.0) + jnp.log1p(jnp.exp(-jnp.abs(x)))


def _rms(x, eps):
    return x * lax.rsqrt(jnp.mean(x * x, axis=-1, keepdims=True) + eps)


def _adaln_kernel(c_ref, w_ref, b_ref, o_ref):
    o_ref[...] = _dot_hi(_silu(c_ref[...]), w_ref[...]) + b_ref[...]


def _adaln(cvec, w, b):
    rows, d = cvec.shape
    n = w.shape[1]
    tn = n // 4
    return pl.pallas_call(
        _adaln_kernel,
        out_shape=jax.ShapeDtypeStruct((rows, n), F32),
        grid=(n // tn,),
        in_specs=[pl.BlockSpec((rows, d), lambda j: (0, 0)),
                  pl.BlockSpec((d, tn), lambda j: (0, j)),
                  pl.BlockSpec((1, tn), lambda j: (0, j))],
        out_specs=pl.BlockSpec((rows, tn), lambda j: (0, j)),
        compiler_params=_cparams("arbitrary"),
        name="adaln",
    )(cvec, w, b.reshape(1, n))


def _inproj_kernel(x_ref, g_ref, mod_ref, w_ref, o_ref):
    h = _rms(x_ref[...], NORM_EPS) * g_ref[...]
    h = h * (1.0 + mod_ref[1:2, :]) + mod_ref[0:1, :]
    o_ref[...] = _dot(h.astype(BF16), w_ref[...])


def _inproj(x, g, mod, w, tm):
    bsz, t, d = x.shape
    n = w.shape[1]
    return pl.pallas_call(
        _inproj_kernel,
        out_shape=jax.ShapeDtypeStruct((bsz, t, n), F32),
        grid=(bsz, t // tm),
        in_specs=[pl.BlockSpec((None, tm, d), lambda b, i: (b, i, 0)),
                  pl.BlockSpec((1, d), lambda b, i: (0, 0)),
                  pl.BlockSpec((None, 6, d), lambda b, i: (b, 0, 0)),
                  pl.BlockSpec((d, n), lambda b, i: (0, 0))],
        out_specs=pl.BlockSpec((None, tm, n), lambda b, i: (b, i, 0)),
        compiler_params=_cparams("parallel", "parallel"),
        name="inproj",
    )(x, g, mod, w)


def _hgrn2_kernel(layer, tc, tl,
                  qc_ref, ffc_ref, fbc_ref, ic_ref, gc_ref,
                  ql_ref, ffl_ref, fbl_ref, il_ref, gl_ref,
                  lbl_ref, ng_ref,
                  oc_ref, ol_ref,
                  q_s, v_s, k_s, cum_s, qe_s, ke_s, et_s, o_s, st_s):
    t = tc + tl
    nbc = tc // GLA_BLOCK
    nb = t // GLA_BLOCK
    pt = PREP_TILE

    lg = lbl_ref[...]
    mx = jnp.max(lg, axis=0, keepdims=True)
    ex = jnp.exp(lg - mx)
    pr = ex / jnp.sum(ex, axis=0, keepdims=True)
    lb = jnp.zeros(lg.shape[1:], F32)
    for j in range(1, layer + 1):
        lb = lb + pr[j]

    ri = lax.broadcasted_iota(jnp.int32, (pt, pt), 0)
    ci = lax.broadcasted_iota(jnp.int32, (pt, pt), 1)
    same = (ri // GLA_BLOCK) == (ci // GLA_BLOCK)
    m_all = jnp.where(same, 1.0, 0.0).astype(BF16)
    m_low = jnp.where(same & (ci <= ri), 1.0, 0.0).astype(BF16)

    def prep(q_ref, ff_ref, fb_ref, i_ref, src, dst):
        q = _silu(q_ref[pl.ds(src, pt), :])
        q_s[pl.ds(dst, pt), :] = q
        v_s[pl.ds(dst, pt), :] = i_ref[pl.ds(src, pt), :]
        for d, z_ref in enumerate((ff_ref, fb_ref)):
            z = z_ref[pl.ds(src, pt), :]
            lbd = lb[d:d + 1, :]
            a = jnp.log(lbd)
            b = jnp.log1p(-lbd) + _log_sigmoid(z)
            lf = jnp.maximum(a, b) + jnp.log1p(jnp.exp(-jnp.abs(a - b)))
            k = (1.0 - lbd) * _sigmoid(-z)
            tot = _dot_m3(m_all, lf)
            cum = _dot_m3(m_low, lf)
            if d == 1:
                cum = tot - cum + lf
            k_s[d, pl.ds(dst, pt), :] = k
            cum_s[d, pl.ds(dst, pt), :] = cum
            qe_s[d, pl.ds(dst, pt), :] = (q * jnp.exp(cum)).astype(BF16)
            ke_s[d, pl.ds(dst, pt), :] = (k * jnp.exp(tot - cum)).astype(BF16)
            et_s[d, pl.ds(dst, pt), :] = jnp.exp(tot)

    for j in range(tc // pt):
        prep(qc_ref, ffc_ref, fbc_ref, ic_ref, j * pt, j * pt)

    def prep_lat(j, carry):
        src = pl.multiple_of(j * pt, pt)
        prep(ql_ref, ffl_ref, fbl_ref, il_ref, src, pl.multiple_of(tc + j * pt, pt))
        return carry

    lax.fori_loop(0, tl // pt, prep_lat, 0)

    st_s[...] = jnp.zeros(st_s.shape, F32)
    rows = lax.broadcasted_iota(jnp.int32, (GLA_BLOCK, LANES), 0)

    def block(d, r0):
        sl = pl.ds(r0, GLA_BLOCK)
        st = st_s[d]
        o = _dot_nt(qe_s[d, sl, :], st.astype(BF16))
        qb = q_s[sl, :]
        kb = k_s[d, sl, :]
        vb = v_s[sl, :]
        cb = cum_s[d, sl, :]
        for s in range(GLA_BLOCK):
            keep = (rows >= s) if d == 0 else (rows <= s)
            dm = jnp.where(keep, cb - cb[s:s + 1, :], -1e30)
            w = jnp.sum(qb * kb[s:s + 1, :] * jnp.exp(dm), axis=1, keepdims=True)
            o = o + w * vb[s:s + 1, :]
        o_s[d, sl, :] = o
        upd = _dot_tn(vb.astype(BF16), ke_s[d, sl, :])
        st_s[d] = st * et_s[d, pl.ds(r0, 1), :] + upd

    def step(i, carry):
        block(0, pl.multiple_of(i * GLA_BLOCK, GLA_BLOCK))
        jb = jnp.where(i < nbc, nbc - 1 - i, nb - 1 - (i - nbc))
        block(1, pl.multiple_of(jb * GLA_BLOCK, GLA_BLOCK))
        return carry

    lax.fori_loop(0, nb, step, 0)

    ng = ng_ref[...]

    def fin(g_ref, o_ref, src, dst):
        o = o_s[0, pl.ds(dst, pt), :] + o_s[1, pl.ds(dst, pt), :]
        y = _rms(o, NORM_EPS) * ng
        o_ref[pl.ds(src, pt), :] = (y * _silu(g_ref[pl.ds(src, pt), :])).astype(o_ref.dtype)

    for j in range(tc // pt):
        fin(gc_ref, oc_ref, j * pt, j * pt)

    def fin_lat(j, carry):
        fin(gl_ref, ol_ref, pl.multiple_of(j * pt, pt), pl.multiple_of(tc + j * pt, pt))
        return carry

    lax.fori_loop(0, tl // pt, fin_lat, 0)


def _hgrn2(p_c, p_l, lb_logits, norm_g, layer):
    bsz, tc, _ = p_c.shape
    tl = p_l.shape[1]
    t = tc + tl
    depth = lb_logits.shape[0]
    nh = HG_HEADS
    w = HG_DK

    def col(tx, j):
        return pl.BlockSpec((None, tx, w), lambda b, h, j=j: (b, 0, j * nh + h))

    in_specs = ([col(tc, j) for j in range(5)] + [col(tl, j) for j in range(5)]
                + [pl.BlockSpec((depth, 2, w), lambda b, h: (0, 0, h)),
                   pl.BlockSpec((1, w), lambda b, h: (0, 0))])
    out_specs = [pl.BlockSpec((None, tc, w), lambda b, h: (b, 0, h)),
                 pl.BlockSpec((None, tl, w), lambda b, h: (b, 0, h))]
    scratch = [pltpu.VMEM((t, w), F32), pltpu.VMEM((t, w), F32),
               pltpu.VMEM((2, t, w), F32), pltpu.VMEM((2, t, w), F32),
               pltpu.VMEM((2, t, w), BF16), pltpu.VMEM((2, t, w), BF16),
               pltpu.VMEM((2, t, w), F32), pltpu.VMEM((2, t, w), F32),
               pltpu.VMEM((2, w, w), F32)]
    return pl.pallas_call(
        functools.partial(_hgrn2_kernel, layer, tc, tl),
        out_shape=[jax.ShapeDtypeStruct((bsz, tc, nh * w), BF16),
                   jax.ShapeDtypeStruct((bsz, tl, nh * w), BF16)],
        grid=(bsz, nh),
        in_specs=in_specs,
        out_specs=out_specs,
        scratch_shapes=scratch,
        compiler_params=_cparams("parallel", "parallel"),
        name="hgrn2",
    )(p_c, p_c, p_c, p_c, p_c, p_l, p_l, p_l, p_l, p_l, lb_logits, norm_g)


def _rwkv_prep_kernel(nt, p_ref, pp_ref, pn_ref, mu_ref, w0_ref, w2f_ref, w2b_ref, a0_ref, a2_ref, g2_ref,
                      kk_ref, ka_ref, hsum_ref,
                      r_ref, wf_ref, wb_ref, k_ref, an_ref, bb_ref, vt_ref, v_ref, g_ref):
    i = pl.program_id(1)
    p = p_ref[...]
    tm = p.shape[0]
    rows = lax.broadcasted_iota(jnp.int32, p.shape, 0)
    first = jnp.where(i > 0, pp_ref[7:8, :], 0.0)
    last = jnp.where(i < nt - 1, pn_ref[0:1, :], 0.0)
    prev = jnp.where(rows == 0, first, pltpu.roll(p, 1, 0))
    nxt = jnp.where(rows == tm - 1, last, pltpu.roll(p, tm - 1, 0))
    xs = p + mu_ref[...] * (0.5 * (prev + nxt) - p)

    c = RW_C
    r = xs[:, 0:c]
    k = xs[:, c:2 * c]
    v = xs[:, 2 * c:3 * c]
    wd = jnp.tanh(xs[:, 3 * c:3 * c + LANES])
    gd = _sigmoid(xs[:, 3 * c + LANES:3 * c + 2 * LANES])
    ad = xs[:, 3 * c + 2 * LANES:3 * c + 3 * LANES]

    a = _sigmoid(a0_ref[...] + _dot_hi(ad, a2_ref[...]))
    kk = k * kk_ref[...]
    nrm = jnp.sqrt(_dot_x3(kk * kk, hsum_ref[...]))
    kk = kk / jnp.maximum(nrm, 1e-12)
    kp = k * (1.0 + (a - 1.0) * ka_ref[...])
    g = _dot_hi(gd, g2_ref[...])

    def decay(w2_ref, d):
        wlog = -_softplus(-(w0_ref[d:d + 1, :] + _dot_hi(wd, w2_ref[...]))) - 0.5
        return jnp.exp(-jnp.exp(wlog))

    vals = (r, decay(w2f_ref, 0), decay(w2b_ref, 1), kp, -kk, kk * a)
    for ref, val in zip((r_ref, wf_ref, wb_ref, k_ref, an_ref, bb_ref), vals):
        for h in range(RW_HEADS):
            ref[h] = val[:, h * RW_HD:(h + 1) * RW_HD]
    vt = v.T
    for h in range(RW_HEADS):
        vt_ref[h] = vt[h * RW_HD:(h + 1) * RW_HD, :]
    v_ref[...] = v
    g_ref[...] = g


def _rwkv_prep(p, consts, tm):
    bsz, t, n = p.shape
    nt = t // tm
    c = RW_C
    hb = tm // 8
    full = lambda a: pl.BlockSpec(a.shape, lambda b, i, nd=a.ndim: (0,) * nd)
    in_specs = ([pl.BlockSpec((None, tm, n), lambda b, i: (b, i, 0)),
                 pl.BlockSpec((None, 8, n), lambda b, i: (b, jnp.maximum(i * hb - 1, 0), 0)),
                 pl.BlockSpec((None, 8, n), lambda b, i: (b, jnp.minimum((i + 1) * hb, t // 8 - 1), 0))]
                + [full(a) for a in consts])
    head = pl.BlockSpec((None, RW_HEADS, tm, RW_HD), lambda b, i: (b, 0, i, 0))
    out_specs = [head] * 6 + [pl.BlockSpec((None, RW_HEADS, RW_HD, tm), lambda b, i: (b, 0, 0, i)),
                              pl.BlockSpec((None, tm, c), lambda b, i: (b, i, 0)),
                              pl.BlockSpec((None, tm, c), lambda b, i: (b, i, 0))]
    hshape = jax.ShapeDtypeStruct((bsz, RW_HEADS, t, RW_HD), F32)
    out_shape = [hshape] * 6 + [jax.ShapeDtypeStruct((bsz, RW_HEADS, RW_HD, t), F32),
                                jax.ShapeDtypeStruct((bsz, t, c), F32),
                                jax.ShapeDtypeStruct((bsz, t, c), F32)]
    return pl.pallas_call(
        functools.partial(_rwkv_prep_kernel, nt),
        out_shape=out_shape,
        grid=(bsz, nt),
        in_specs=in_specs,
        out_specs=out_specs,
        compiler_params=_cparams("parallel", "parallel"),
        name="rwkv_prep",
    )(p, p, p, *consts)


def _rwkv_scan_kernel(*refs):
    dir_in = (refs[0:6], refs[6:12])
    s0_ref = refs[12]
    y_refs = refs[13:15]
    sout_ref = refs[15]
    s_scr, vt_scr, yb_scr = refs[16:]
    nh = RW_HEADS
    sb = SCAN_BLOCK
    i = pl.program_id(1)
    nsub = y_refs[0].shape[-1] // sb
    lane = lax.broadcasted_iota(jnp.int32, (RW_HD, sb), 1)

    @pl.when(i == 0)
    def _():
        s_scr[...] = s0_ref[...]
        yb_scr[...] = jnp.zeros(yb_scr.shape, F32)

    def sub(u, carry):
        t0 = (pl.multiple_of(u * sb, sb), pl.multiple_of((nsub - 1 - u) * sb, sb))
        for d in range(2):
            for h in range(nh):
                vt_scr[d, h] = dir_in[d][5][h, :, pl.ds(t0[d], sb)]

        def step(j, carry2):
            for d in range(2):
                r_ref, w_ref, k_ref, an_ref, bb_ref, _ = dir_in[d]
                t = t0[d] + (j if d == 0 else sb - 1 - j)
                edge = 0 if d == 0 else sb - 1
                shift = sb - 1 if d == 0 else 1
                for h in range(nh):
                    row = lambda ref: ref[h, pl.ds(t, 1), :]
                    s = s_scr[d, h]
                    cur = vt_scr[d, h]
                    vcol = cur[:, edge:edge + 1]
                    vt_scr[d, h] = pltpu.roll(cur, shift, 1)
                    sa = jnp.sum(s * row(an_ref), axis=1, keepdims=True)
                    s = s * row(w_ref) + sa * row(bb_ref) + vcol * row(k_ref)
                    s_scr[d, h] = s
                    y = jnp.sum(s * row(r_ref), axis=1, keepdims=True)
                    yb = pltpu.roll(yb_scr[d, h], shift, 1)
                    yb_scr[d, h] = jnp.where(lane == sb - 1 - edge, y, yb)
            return carry2

        lax.fori_loop(0, sb, step, 0)
        for d in range(2):
            for h in range(nh):
                y_refs[d][h, :, pl.ds(t0[d], sb)] = yb_scr[d, h]
        return carry

    lax.fori_loop(0, nsub, sub, 0)

    @pl.when(i == pl.num_programs(1) - 1)
    def _():
        sout_ref[...] = s_scr[...]


def _rwkv_scan(prep, s0, tb):
    r, wf, wb, k, an, bb, vt = prep[:7]
    bsz, nh, t, hd = r.shape
    nblk = t // tb
    fwd = lambda b, i: (b, 0, i, 0)
    bwd = lambda b, i: (b, 0, nblk - 1 - i, 0)
    fwd_t = lambda b, i: (b, 0, 0, i)
    bwd_t = lambda b, i: (b, 0, 0, nblk - 1 - i)
    row = lambda im: pl.BlockSpec((None, nh, tb, hd), im)
    col = lambda im: pl.BlockSpec((None, nh, hd, tb), im)
    sspec = pl.BlockSpec((None, 2, nh, hd, hd), lambda b, i: (b, 0, 0, 0, 0))
    yshape = jax.ShapeDtypeStruct((bsz, nh, hd, t), F32)
    return pl.pallas_call(
        _rwkv_scan_kernel,
        out_shape=[yshape, yshape, jax.ShapeDtypeStruct(s0.shape, F32)],
        grid=(bsz, nblk),
        in_specs=[row(fwd)] * 5 + [col(fwd_t)] + [row(bwd)] * 5 + [col(bwd_t)] + [sspec],
        out_specs=[col(fwd_t), col(bwd_t), sspec],
        scratch_shapes=[pltpu.VMEM((2, nh, hd, hd), F32),
                        pltpu.VMEM((2, nh, hd, SCAN_BLOCK), F32),
                        pltpu.VMEM((2, nh, hd, SCAN_BLOCK), F32)],
        compiler_params=_cparams("parallel", "arbitrary"),
        name="rwkv_scan",
    )(r, wf, k, an, bb, vt, r, wb, k, an, bb, vt, s0)


def _rwkv_fin_kernel(yf_ref, yb_ref, r_ref, k_ref, v_ref, g_ref, rk_ref, lng_ref, lnb_ref, hmean_ref, hsum_ref, o_ref):
    nh = RW_HEADS
    yt = jnp.concatenate([yf_ref[h] + yb_ref[h] for h in range(nh)], axis=0)
    y = yt.T
    r = jnp.concatenate([r_ref[h] for h in range(nh)], axis=1)
    k = jnp.concatenate([k_ref[h] for h in range(nh)], axis=1)
    v = v_ref[...]
    mean = _dot_x3(y, hmean_ref[...])
    yc = y - mean
    var = _dot_x3(yc * yc, hmean_ref[...])
    yn = yc * lax.rsqrt(var + RW_LN_EPS) * lng_ref[...] + lnb_ref[...]
    bonus = _dot_x3(r * k * rk_ref[...], hsum_ref[...]) * v
    o_ref[...] = ((yn + bonus) * g_ref[...]).astype(o_ref.dtype)


def _rwkv_fin(yf, yb, prep, rk, lng, lnb, hmean, hsum, tm):
    bsz, nh, hd, t = yf.shape
    c = nh * hd
    r, _, _, k, _, _, _, v, g = prep
    yspec = pl.BlockSpec((None, nh, hd, tm), lambda b, i: (b, 0, 0, i))
    hspec = pl.BlockSpec((None, nh, tm, hd), lambda b, i: (b, 0, i, 0))
    tspec = pl.BlockSpec((None, tm, c), lambda b, i: (b, i, 0))
    vec = pl.BlockSpec((1, c), lambda b, i: (0, 0))
    mat = pl.BlockSpec((c, c), lambda b, i: (0, 0))
    return pl.pallas_call(
        _rwkv_fin_kernel,
        out_shape=jax.ShapeDtypeStruct((bsz, t, c), BF16),
        grid=(bsz, t // tm),
        in_specs=[yspec, yspec, hspec, hspec, tspec, tspec, vec, vec, vec, mat, mat],
        out_specs=tspec,
        compiler_params=_cparams("parallel", "parallel"),
        name="rwkv_fin",
    )(yf, yb, r, k, v, g, rk, lng, lnb, hmean, hsum)


def _rope(x, cos, sin):
    lane = lax.broadcasted_iota(jnp.int32, x.shape, 1)
    partner = jnp.where((lane % ROPE_AXIS_DIM) < ROPE_AXIS_DIM // 2,
                        pltpu.roll(x, LANES - ROPE_AXIS_DIM // 2, 1), pltpu.roll(x, ROPE_AXIS_DIM // 2, 1))
    return x * cos + partner * sin


def _lambda(lam_ref, lam_init):
    lp = lam_ref[...]
    s1 = jnp.sum(lp[0:1, :] * lp[1:2, :], axis=1, keepdims=True)
    s2 = jnp.sum(lp[2:3, :] * lp[3:4, :], axis=1, keepdims=True)
    return jnp.exp(s1) - jnp.exp(s2) + lam_init


def _diff_attend(q, k, v, lam, sg, lam_init):
    lane = lax.broadcasted_iota(jnp.int32, q.shape, 1)
    qb = q.astype(BF16)
    zero = jnp.zeros_like(qb)
    probs = []
    for m in range(2):
        qm = jnp.where((lane // DA_HD) == m, qb, zero)
        s = _dot_nt(qm, k)
        e = jnp.exp(s - jnp.max(s, axis=-1, keepdims=True))
        probs.append(e / jnp.sum(e, axis=-1, keepdims=True))
    w = probs[0] - lam * probs[1]
    o = _dot(w.astype(BF16), v)
    return _rms(o, DA_SUBLN_EPS) * sg * (1.0 - lam_init)


def _da_lat_kernel(lam_init, tc, tl, q_ref, kc_ref, vc_ref, kl_ref, vl_ref, cos_ref, sin_ref, lam_ref, sg_ref,
                   o_ref, k_s, v_s):
    qi = pl.program_id(2)
    tq = q_ref.shape[0]

    @pl.when(qi == 0)
    def _():
        k_s[0:tc, :] = kc_ref[...].astype(BF16)
        v_s[0:tc, :] = vc_ref[...].astype(BF16)
        k_s[tc:tc + tl, :] = _rope(kl_ref[...], cos_ref[...], sin_ref[...]).astype(BF16)
        v_s[tc:tc + tl, :] = vl_ref[...].astype(BF16)

    r0 = pl.multiple_of(qi * tq, tq)
    q = _rope(q_ref[...], cos_ref[pl.ds(r0, tq), :], sin_ref[pl.ds(r0, tq), :]) * DA_SCALE
    o = _diff_attend(q, k_s[...], v_s[...], _lambda(lam_ref, lam_init), sg_ref[...], lam_init)
    o_ref[...] = o.astype(o_ref.dtype)


def _da_ctx_kernel(lam_init, q_ref, k_ref, v_ref, lam_ref, sg_ref, o_ref):
    o = _diff_attend(q_ref[...] * DA_SCALE, k_ref[...].astype(BF16), v_ref[...].astype(BF16),
                     _lambda(lam_ref, lam_init), sg_ref[...], lam_init)
    o_ref[...] = o.astype(o_ref.dtype)


def _diff_attn(p_c, p_l, cos, sin, lam_p, sg, layer, need_ctx, tq):
    bsz, tc, _ = p_c.shape
    tl = p_l.shape[1]
    nh = DA_HEADS
    w = 2 * DA_HD
    lam_init = 0.8 - 0.6 * math.exp(-0.3 * layer)
    small = [pl.BlockSpec(lam_p.shape, lambda *a: (0, 0)), pl.BlockSpec((1, w), lambda *a: (0, 0))]
    o_l = pl.pallas_call(
        functools.partial(_da_lat_kernel, lam_init, tc, tl),
        out_shape=jax.ShapeDtypeStruct((bsz, tl, nh * w), BF16),
        grid=(bsz, nh, tl // tq),
        in_specs=[pl.BlockSpec((None, tq, w), lambda b, h, i: (b, i, h)),
                  pl.BlockSpec((None, tc, w), lambda b, h, i: (b, 0, nh + h)),
                  pl.BlockSpec((None, tc, w), lambda b, h, i: (b, 0, 2 * nh + h)),
                  pl.BlockSpec((None, tl, w), lambda b, h, i: (b, 0, nh + h)),
                  pl.BlockSpec((None, tl, w), lambda b, h, i: (b, 0, 2 * nh + h)),
                  pl.BlockSpec((tl, w), lambda b, h, i: (0, 0)),
                  pl.BlockSpec((tl, w), lambda b, h, i: (0, 0))] + small,
        out_specs=pl.BlockSpec((None, tq, w), lambda b, h, i: (b, i, h)),
        scratch_shapes=[pltpu.VMEM((tc + tl, w), BF16), pltpu.VMEM((tc + tl, w), BF16)],
        compiler_params=_cparams("parallel", "parallel", "arbitrary"),
        name="diff_attn_lat",
    )(p_l, p_c, p_c, p_l, p_l, cos, sin, lam_p, sg)
    o_c = None
    if need_ctx:
        o_c = pl.pallas_call(
            functools.partial(_da_ctx_kernel, lam_init),
            out_shape=jax.ShapeDtypeStruct((bsz, tc, nh * w), BF16),
            grid=(bsz, nh),
            in_specs=[pl.BlockSpec((None, tc, w), lambda b, h: (b, 0, h)),
                      pl.BlockSpec((None, tc, w), lambda b, h: (b, 0, nh + h)),
                      pl.BlockSpec((None, tc, w), lambda b, h: (b, 0, 2 * nh + h))] + small,
            out_specs=pl.BlockSpec((None, tc, w), lambda b, h: (b, 0, h)),
            compiler_params=_cparams("parallel", "parallel"),
            name="diff_attn_ctx",
        )(p_c, p_c, p_c, lam_p, sg)
    return o_l, o_c


def _rope_tables(tl):
    rows = tl // GRID_W
    t = np.arange(tl)
    pos = np.stack([t // GRID_W, t % GRID_W], axis=1).astype(np.float32)
    inv_freq = (1.0 / (ROPE_BASE ** (jnp.arange(0, ROPE_AXIS_DIM, 2, dtype=F32) / ROPE_AXIS_DIM)))
    lane = np.arange(LANES)
    d = lane % DA_HD
    axis = d // ROPE_AXIS_DIM
    freq = d % (ROPE_AXIS_DIM // 2)
    sign = np.where((d % ROPE_AXIS_DIM) < ROPE_AXIS_DIM // 2, -1.0, 1.0).astype(np.float32)
    ang = jnp.asarray(pos)[:, axis] * inv_freq[freq][None, :]
    del rows
    return jnp.cos(ang), jnp.sin(ang) * sign[None, :]


def _merge_kernel(x_ref, mod_ref, fh_ref, fr_ref, fd_ref, gt_ref, ph_ref, pr_ref, pd_ref, wo_ref, o_ref):
    d = x_ref.shape[-1]
    g = gt_ref[...]
    m = (_sigmoid(g[:, 0:d]) * _dot(fh_ref[...], ph_ref[...])
         + _sigmoid(g[:, d:2 * d]) * _dot(fr_ref[...], pr_ref[...])
         + _sigmoid(g[:, 2 * d:3 * d]) * _dot(fd_ref[...], pd_ref[...]))
    o_ref[...] = x_ref[...] + mod_ref[2:3, :] * _dot(m.astype(BF16), wo_ref[...])


def _merge(x, mod, fh, fr, fd, gate, ph, pr, pd, wo, tm):
    bsz, t, d = x.shape
    c = fh.shape[-1]
    row = lambda n: pl.BlockSpec((None, tm, n), lambda b, i: (b, i, 0))
    wspec = lambda a: pl.BlockSpec(a.shape, lambda b, i: (0, 0))
    return pl.pallas_call(
        _merge_kernel,
        out_shape=jax.ShapeDtypeStruct(x.shape, F32),
        grid=(bsz, t // tm),
        in_specs=[row(d), pl.BlockSpec((None, 6, d), lambda b, i: (b, 0, 0)), row(c), row(c), row(c), row(3 * d),
                  wspec(ph), wspec(pr), wspec(pd), wspec(wo)],
        out_specs=row(d),
        input_output_aliases={0: 0},
        compiler_params=_cparams("parallel", "parallel"),
        name="merge",
    )(x, mod, fh, fr, fd, gate, ph, pr, pd, wo)


def _ffn_prologue(x_ref, g_ref, mod_ref):
    h = _rms(x_ref[...], NORM_EPS) * g_ref[...]
    return h * (1.0 + mod_ref[4:5, :]) + mod_ref[3:4, :]


def _ffn_epilogue(x_ref, mod_ref, acc, fg_ref, o_ref):
    y = x_ref[...] + mod_ref[5:6, :] * acc
    if fg_ref is not None:
        y = _rms(y, NORM_EPS) * fg_ref[...]
    o_ref[...] = y


def _swiglu_step(h, w1_ref, w3_ref, w2_ref):
    a = _dot(h, w1_ref[...])
    u = _silu(a) * _dot(h, w3_ref[...])
    return _dot(u.astype(BF16), w2_ref[...])


def _ffn_kernel(final, x_ref, g_ref, mod_ref, w1_ref, w3_ref, w2_ref, *rest):
    fg_ref, (o_ref, h_s, acc_s) = (rest[0], rest[1:]) if final else (None, rest)
    f = pl.program_id(2)

    @pl.when(f == 0)
    def _():
        h_s[...] = _ffn_prologue(x_ref, g_ref, mod_ref).astype(BF16)
        acc_s[...] = jnp.zeros(acc_s.shape, F32)

    acc_s[...] += _swiglu_step(h_s[...], w1_ref, w3_ref, w2_ref)

    @pl.when(f == pl.num_programs(2) - 1)
    def _():
        _ffn_epilogue(x_ref, mod_ref, acc_s[...], fg_ref, o_ref)


def _ffn(x, g, mod, w1, w3, w2, final_g, tm, tf):
    bsz, t, d = x.shape
    fdim = w1.shape[1]
    final = final_g is not None
    in_specs = [pl.BlockSpec((None, tm, d), lambda b, i, f: (b, i, 0)),
                pl.BlockSpec((1, d), lambda b, i, f: (0, 0)),
                pl.BlockSpec((None, 6, d), lambda b, i, f: (b, 0, 0)),
                pl.BlockSpec((d, tf), lambda b, i, f: (0, f)),
                pl.BlockSpec((d, tf), lambda b, i, f: (0, f)),
                pl.BlockSpec((tf, d), lambda b, i, f: (f, 0))]
    args = [x, g, mod, w1, w3, w2]
    if final:
        in_specs.append(pl.BlockSpec((1, d), lambda b, i, f: (0, 0)))
        args.append(final_g)
    return pl.pallas_call(
        functools.partial(_ffn_kernel, final),
        out_shape=jax.ShapeDtypeStruct(x.shape, F32),
        grid=(bsz, t // tm, fdim // tf),
        in_specs=in_specs,
        out_specs=pl.BlockSpec((None, tm, d), lambda b, i, f: (b, i, 0)),
        scratch_shapes=[pltpu.VMEM((tm, d), BF16), pltpu.VMEM((tm, d), F32)],
        compiler_params=_cparams("parallel", "parallel", "arbitrary"),
        name="ffn",
    )(*args)


def _moe_kernel(final, x_ref, g_ref, mod_ref, rt_ref, w1_ref, w3_ref, w2_ref, *rest):
    fg_ref, (o_ref, h_s, acc_s, cmb_s) = (rest[0], rest[1:]) if final else (None, rest)
    e = pl.program_id(2)
    f = pl.program_id(3)
    lane = lax.broadcasted_iota(jnp.int32, cmb_s.shape, 1).astype(F32)

    @pl.when((e == 0) & (f == 0))
    def _():
        h = _ffn_prologue(x_ref, g_ref, mod_ref)
        h_s[...] = h.astype(BF16)
        acc_s[...] = jnp.zeros(acc_s.shape, F32)
        logits = jnp.where(lane < N_EXPERTS, _dot_hi(h, rt_ref[...]), -jnp.inf)
        m1 = jnp.max(logits, axis=-1, keepdims=True)
        i1 = jnp.min(jnp.where(logits == m1, lane, float(LANES)), axis=-1, keepdims=True)
        rest_l = jnp.where(lane == i1, -jnp.inf, logits)
        m2 = jnp.max(rest_l, axis=-1, keepdims=True)
        i2 = jnp.min(jnp.where(rest_l == m2, lane, float(LANES)), axis=-1, keepdims=True)
        e2 = jnp.exp(m2 - m1)
        cmb_s[...] = jnp.where(lane == i1, 1.0 / (1.0 + e2), 0.0) + jnp.where(lane == i2, e2 / (1.0 + e2), 0.0)

    ce = jnp.sum(jnp.where(lane == e.astype(F32), cmb_s[...], 0.0), axis=-1, keepdims=True)
    acc_s[...] += ce * _swiglu_step(h_s[...], w1_ref, w3_ref, w2_ref)

    @pl.when((e == pl.num_programs(2) - 1) & (f == pl.num_programs(3) - 1))
    def _():
        _ffn_epilogue(x_ref, mod_ref, acc_s[...], fg_ref, o_ref)


def _moe(x, g, mod, router, w1, w3, w2, final_g, tm, tf):
    bsz, t, d = x.shape
    ne, _, fdim = w1.shape
    final = final_g is not None
    in_specs = [pl.BlockSpec((None, tm, d), lambda b, i, e, f: (b, i, 0)),
                pl.BlockSpec((1, d), lambda b, i, e, f: (0, 0)),
                pl.BlockSpec((None, 6, d), lambda b, i, e, f: (b, 0, 0)),
                pl.BlockSpec((d, LANES), lambda b, i, e, f: (0, 0)),
                pl.BlockSpec((None, d, tf), lambda b, i, e, f: (e, 0, f)),
                pl.BlockSpec((None, d, tf), lambda b, i, e, f: (e, 0, f)),
                pl.BlockSpec((None, tf, d), lambda b, i, e, f: (e, f, 0))]
    args = [x, g, mod, router, w1, w3, w2]
    if final:
        in_specs.append(pl.BlockSpec((1, d), lambda b, i, e, f: (0, 0)))
        args.append(final_g)
    return pl.pallas_call(
        functools.partial(_moe_kernel, final),
        out_shape=jax.ShapeDtypeStruct(x.shape, F32),
        grid=(bsz, t // tm, ne, fdim // tf),
        in_specs=in_specs,
        out_specs=pl.BlockSpec((None, tm, d), lambda b, i, e, f: (b, i, 0)),
        scratch_shapes=[pltpu.VMEM((tm, d), BF16), pltpu.VMEM((tm, d), F32), pltpu.VMEM((tm, LANES), F32)],
        compiler_params=_cparams("parallel", "parallel", "arbitrary", "arbitrary"),
        name="moe",
    )(*args)


def _row_tile(t, pref):
    tm = min(pref, t)
    while t % tm:
        tm //= 2
    return tm


def kernel(x, c, ctx, c_ctx, ada_w, ada_b, norm_mix_g, norm_ffn_g, final_norm_g, w_in, hg_lb_logits, hg_norm_g,
           hg_proj, rw_mu, rw_w0, rw_w2, rw_a0, rw_a2, rw_g2, rw_k_k, rw_k_a, rw_r_k, rw_ln_g, rw_ln_b, rw_proj,
           da_lambda, da_subln_g, da_proj, w_out, ffn_w1, ffn_w3, ffn_w2, moe_router, moe_w1, moe_w3, moe_w2):
    bsz, tl, d = x.shape
    tc = ctx.shape[1]
    depth = ada_w.shape[0]
    hg_cols = 5 * HG_HEADS * HG_DK
    rw_cols = 3 * RW_C + 2 * RW_W_RANK + RW_A_RANK + RW_G_RANK
    da_cols = 3 * DA_HEADS * 2 * DA_HD
    o_rw = hg_cols
    o_da = o_rw + rw_cols
    o_gt = o_da + da_cols

    c3 = 3 * RW_C
    rw_perm = np.concatenate([np.arange(0, c3 + 2 * RW_W_RANK),
                              np.arange(c3 + 2 * RW_W_RANK + RW_A_RANK, rw_cols),
                              np.arange(c3 + 2 * RW_W_RANK, c3 + 2 * RW_W_RANK + RW_A_RANK)])
    rw_pad = (-rw_cols) % LANES

    cvec = jnp.zeros((16, d), F32).at[:bsz].set(c).at[bsz].set(c_ctx)
    cos, sin = _rope_tables(tl)
    head_id = np.arange(RW_C) // RW_HD
    hsum = jnp.asarray((head_id[:, None] == head_id[None, :]).astype(np.float32), BF16)
    hmean = (hsum.astype(F32) / RW_HD).astype(BF16)
    zpad = lambda a, rows: jnp.concatenate([a, jnp.zeros((rows - a.shape[0],) + a.shape[1:], a.dtype)], axis=0)

    tm_l = _row_tile(tl, 512)
    tm_c = _row_tile(tc, 256)
    xl, xc = x, ctx
    for l in range(depth):
        need_ctx = l < depth - 1
        last = l == depth - 1
        mods = _adaln(cvec, ada_w[l], ada_b[l])
        mod_l = mods[:bsz].reshape(bsz, 6, d)
        mod_c = jnp.broadcast_to(mods[bsz].reshape(1, 6, d), (bsz, 6, d))

        wl = w_in[l]
        w_hg = wl[:, :o_rw].astype(BF16)
        w_rw = jnp.pad(wl[:, o_rw:o_da][:, rw_perm], ((0, 0), (0, rw_pad))).astype(BF16)
        w_da = wl[:, o_da:o_gt].astype(BF16)
        w_gt = wl[:, o_gt:].astype(BF16)
        gmix = norm_mix_g[l].reshape(1, d)
        proj = lambda xx, mod, w, tm: _inproj(xx, gmix, mod, w, tm)

        fh_c, fh_l = _hgrn2(proj(xc, mod_c, w_hg, tm_c), proj(xl, mod_l, w_hg, tm_l),
                            hg_lb_logits, hg_norm_g[l].reshape(1, HG_DK), l)

        mu = jnp.pad(rw_mu[l][rw_perm], (0, rw_pad)).reshape(1, -1)
        consts = (mu, rw_w0[l], zpad(rw_w2[l, 0], LANES),
                  jnp.concatenate([jnp.zeros_like(rw_w2[l, 1]), rw_w2[l, 1]], axis=0),
                  rw_a0[l].reshape(1, RW_C), zpad(rw_a2[l], LANES), rw_g2[l],
                  rw_k_k[l].reshape(1, RW_C), rw_k_a[l].reshape(1, RW_C), hsum)
        prep_c = _rwkv_prep(proj(xc, mod_c, w_rw, tm_c), consts, tm_c)
        prep_l = _rwkv_prep(proj(xl, mod_l, w_rw, tm_l), consts, _row_tile(tl, 256))
        s0 = jnp.zeros((bsz, 2, RW_HEADS, RW_HD, RW_HD), F32)
        yc_f, yc_b, s_ctx = _rwkv_scan(prep_c, s0, _row_tile(tc, 256))
        yl_f, yl_b, _ = _rwkv_scan(prep_l, s_ctx, _row_tile(tl, 256))
        fin = lambda yf, yb, prep, tm: _rwkv_fin(yf, yb, prep, rw_r_k[l].reshape(1, RW_C),
                                                 rw_ln_g[l].reshape(1, RW_C), rw_ln_b[l].reshape(1, RW_C),
                                                 hmean, hsum, tm)
        fr_l = fin(yl_f, yl_b, prep_l, _row_tile(tl, 256))

        pda_c = proj(xc, mod_c, w_da, tm_c)
        pda_l = proj(xl, mod_l, w_da, tm_l)
        fd_l, fd_c = _diff_attn(pda_c, pda_l, cos, sin, da_lambda[l], da_subln_g[l].reshape(1, -1), l, need_ctx,
                                _row_tile(tl, 256))

        projs = (hg_proj[l].astype(BF16), rw_proj[l].astype(BF16), da_proj[l].astype(BF16), w_out[l].astype(BF16))
        xl = _merge(xl, mod_l, fh_l, fr_l, fd_l, proj(xl, mod_l, w_gt, tm_l), *projs, tm_l)
        if need_ctx:
            fr_c = fin(yc_f, yc_b, prep_c, tm_c)
            xc = _merge(xc, mod_c, fh_c, fr_c, fd_c, proj(xc, mod_c, w_gt, tm_c), *projs, tm_c)

        gffn = norm_ffn_g[l].reshape(1, d)
        fg = final_norm_g.reshape(1, d) if last else None
        j = l // 2
        if l % 2 == 0:
            w1, w3, w2 = ffn_w1[j].astype(BF16), ffn_w3[j].astype(BF16), ffn_w2[j].astype(BF16)
            tf = w1.shape[1] // 2
            xl = _ffn(xl, gffn, mod_l, w1, w3, w2, fg, tm_l, tf)
            if need_ctx:
                xc = _ffn(xc, gffn, mod_c, w1, w3, w2, None, tm_c, tf)
        else:
            w1, w3, w2 = moe_w1[j].astype(BF16), moe_w3[j].astype(BF16), moe_w2[j].astype(BF16)
            tf = w1.shape[2] // 2
            router = jnp.pad(moe_router[j], ((0, 0), (0, LANES - N_EXPERTS)))
            xl = _moe(xl, gffn, mod_l, router, w1, w3, w2, fg, tm_l, tf)
            if need_ctx:
                xc = _moe(xc, gffn, mod_c, router, w1, w3, w2, None, tm_c, tf)
    if depth == 0:
        raise ValueError("depth must be positive")
    return xl
```

```python
import functools
import math

import numpy as np
import jax
import jax.numpy as jnp
from jax import lax
from jax.experimental import pallas as pl
from jax.experimental.pallas import tpu as pltpu

F32 = jnp.float32
BF16 = jnp.bfloat16

GRID_W = 64
HG_HEADS = 4
HG_DK = 128
RW_HEADS = 8
RW_HD = 64
RW_C = RW_HEADS * RW_HD
RW_W_RANK = 64
RW_A_RANK = 64
RW_G_RANK = 128
RW_LN_EPS = 64e-5
DA_HEADS = 4
DA_HD = 64
DA_SUBLN_EPS = 1e-5
DA_SCALE = DA_HD ** -0.5
ROPE_AXIS_DIM = DA_HD // 2
ROPE_BASE = 10000.0
N_EXPERTS = 8
NORM_EPS = 1e-6

LANES = 128
GLA_BLOCK = 16
PREP_TILE = 256
RW_CHUNK = 16
VMEM_LIMIT = 52 * 1024 * 1024


def _cparams(*sem):
    return pltpu.CompilerParams(dimension_semantics=sem, vmem_limit_bytes=VMEM_LIMIT)


def _dot(a, b):
    return jnp.dot(a, b, preferred_element_type=F32)


def _dot_nt(a, b):
    return lax.dot_general(a, b, (((1,), (1,)), ((), ())), preferred_element_type=F32)


def _dot_tn(a, b):
    return lax.dot_general(a, b, (((0,), (0,)), ((), ())), preferred_element_type=F32)


def _split2(x):
    hi = x.astype(BF16)
    lo = (x - hi.astype(F32)).astype(BF16)
    return hi, lo


def _split3(x):
    hi = x.astype(BF16)
    r = x - hi.astype(F32)
    mid = r.astype(BF16)
    lo = (r - mid.astype(F32)).astype(BF16)
    return hi, mid, lo


def _dot_hi(a, b):
    ah, al = _split2(a)
    bh, bl = _split2(b)
    return _dot(ah, bh) + (_dot(ah, bl) + _dot(al, bh))


def _dot_x3(x, m):
    hi, mid, lo = _split3(x)
    return _dot(hi, m) + (_dot(mid, m) + _dot(lo, m))


def _dot_m3(m, x):
    hi, mid, lo = _split3(x)
    return _dot(m, hi) + (_dot(m, mid) + _dot(m, lo))


def _sigmoid(x):
    return jax.nn.sigmoid(x)


def _silu(x):
    return x * jax.nn.sigmoid(x)


def _log_sigmoid(z):
    return jnp.minimum(z, 0.0) - jnp.log1p(jnp.exp(-jnp.abs(z)))


def _softplus(x):
    return jnp.maximum(x, 0.0) + jnp.log1p(jnp.exp(-jnp.abs(x)))


def _rms(x, eps):
    return x * lax.rsqrt(jnp.mean(x * x, axis=-1, keepdims=True) + eps)


def _adaln_kernel(c_ref, w_ref, b_ref, o_ref):
    o_ref[...] = _dot_hi(_silu(c_ref[...]), w_ref[...]) + b_ref[...]


def _adaln(cvec, w, b):
    rows, d = cvec.shape
    n = w.shape[1]
    tn = n // 4
    return pl.pallas_call(
        _adaln_kernel,
        out_shape=jax.ShapeDtypeStruct((rows, n), F32),
        grid=(n // tn,),
        in_specs=[pl.BlockSpec((rows, d), lambda j: (0, 0)),
                  pl.BlockSpec((d, tn), lambda j: (0, j)),
                  pl.BlockSpec((1, tn), lambda j: (0, j))],
        out_specs=pl.BlockSpec((rows, tn), lambda j: (0, j)),
        compiler_params=_cparams("arbitrary"),
        name="adaln",
    )(cvec, w, b.reshape(1, n))


def _inproj_kernel(x_ref, g_ref, mod_ref, w_ref, o_ref):
    h = _rms(x_ref[...], NORM_EPS) * g_ref[...]
    h = h * (1.0 + mod_ref[1:2, :]) + mod_ref[0:1, :]
    o_ref[...] = _dot(h.astype(BF16), w_ref[...])


def _inproj(x, g, mod, w, tm):
    bsz, t, d = x.shape
    n = w.shape[1]
    return pl.pallas_call(
        _inproj_kernel,
        out_shape=jax.ShapeDtypeStruct((bsz, t, n), F32),
        grid=(bsz, t // tm),
        in_specs=[pl.BlockSpec((None, tm, d), lambda b, i: (b, i, 0)),
                  pl.BlockSpec((1, d), lambda b, i: (0, 0)),
                  pl.BlockSpec((None, 6, d), lambda b, i: (b, 0, 0)),
                  pl.BlockSpec((d, n), lambda b, i: (0, 0))],
        out_specs=pl.BlockSpec((None, tm, n), lambda b, i: (b, i, 0)),
        compiler_params=_cparams("parallel", "parallel"),
        name="inproj",
    )(x, g, mod, w)


def _hgrn2_kernel(layer, tc, tl,
                  qc_ref, ffc_ref, fbc_ref, ic_ref, gc_ref,
                  ql_ref, ffl_ref, fbl_ref, il_ref, gl_ref,
                  lbl_ref, ng_ref,
                  oc_ref, ol_ref,
                  q_s, v_s, k_s, cum_s, qe_s, ke_s, et_s, o_s, st_s):
    t = tc + tl
    nbc = tc // GLA_BLOCK
    nb = t // GLA_BLOCK
    pt = PREP_TILE

    lg = lbl_ref[...]
    mx = jnp.max(lg, axis=0, keepdims=True)
    ex = jnp.exp(lg - mx)
    pr = ex / jnp.sum(ex, axis=0, keepdims=True)
    lb = jnp.zeros(lg.shape[1:], F32)
    for j in range(1, layer + 1):
        lb = lb + pr[j]

    ri = lax.broadcasted_iota(jnp.int32, (pt, pt), 0)
    ci = lax.broadcasted_iota(jnp.int32, (pt, pt), 1)
    same = (ri // GLA_BLOCK) == (ci // GLA_BLOCK)
    m_all = jnp.where(same, 1.0, 0.0).astype(BF16)
    m_low = jnp.where(same & (ci <= ri), 1.0, 0.0).astype(BF16)

    def prep(q_ref, ff_ref, fb_ref, i_ref, src, dst):
        q = _silu(q_ref[pl.ds(src, pt), :])
        q_s[pl.ds(dst, pt), :] = q
        v_s[pl.ds(dst, pt), :] = i_ref[pl.ds(src, pt), :]
        for d, z_ref in enumerate((ff_ref, fb_ref)):
            z = z_ref[pl.ds(src, pt), :]
            lbd = lb[d:d + 1, :]
            a = jnp.log(lbd)
            b = jnp.log1p(-lbd) + _log_sigmoid(z)
            lf = jnp.maximum(a, b) + jnp.log1p(jnp.exp(-jnp.abs(a - b)))
            k = (1.0 - lbd) * _sigmoid(-z)
            tot = _dot_m3(m_all, lf)
            cum = _dot_m3(m_low, lf)
            if d == 1:
                cum = tot - cum + lf
            k_s[d, pl.ds(dst, pt), :] = k
            cum_s[d, pl.ds(dst, pt), :] = cum
            qe_s[d, pl.ds(dst, pt), :] = (q * jnp.exp(cum)).astype(BF16)
            ke_s[d, pl.ds(dst, pt), :] = (k * jnp.exp(tot - cum)).astype(BF16)
            et_s[d, pl.ds(dst, pt), :] = jnp.exp(tot)

    for j in range(tc // pt):
        prep(qc_ref, ffc_ref, fbc_ref, ic_ref, j * pt, j * pt)

    def prep_lat(j, carry):
        src = pl.multiple_of(j * pt, pt)
        prep(ql_ref, ffl_ref, fbl_ref, il_ref, src, pl.multiple_of(tc + j * pt, pt))
        return carry

    lax.fori_loop(0, tl // pt, prep_lat, 0)

    st_s[...] = jnp.zeros(st_s.shape, F32)
    rows = lax.broadcasted_iota(jnp.int32, (GLA_BLOCK, LANES), 0)

    def block(d, r0):
        sl = pl.ds(r0, GLA_BLOCK)
        st = st_s[d]
        o = _dot_nt(qe_s[d, sl, :], st.astype(BF16))
        qb = q_s[sl, :]
        kb = k_s[d, sl, :]
        vb = v_s[sl, :]
        cb = cum_s[d, sl, :]
        for s in range(GLA_BLOCK):
            keep = (rows >= s) if d == 0 else (rows <= s)
            dm = jnp.where(keep, cb - cb[s:s + 1, :], -1e30)
            w = jnp.sum(qb * kb[s:s + 1, :] * jnp.exp(dm), axis=1, keepdims=True)
            o = o + w * vb[s:s + 1, :]
        o_s[d, sl, :] = o
        upd = _dot_tn(vb.astype(BF16), ke_s[d, sl, :])
        st_s[d] = st * et_s[d, pl.ds(r0, 1), :] + upd

    def step(i, carry):
        block(0, pl.multiple_of(i * GLA_BLOCK, GLA_BLOCK))
        jb = jnp.where(i < nbc, nbc - 1 - i, nb - 1 - (i - nbc))
        block(1, pl.multiple_of(jb * GLA_BLOCK, GLA_BLOCK))
        return carry

    lax.fori_loop(0, nb, step, 0)

    ng = ng_ref[...]

    def fin(g_ref, o_ref, src, dst):
        o = o_s[0, pl.ds(dst, pt), :] + o_s[1, pl.ds(dst, pt), :]
        y = _rms(o, NORM_EPS) * ng
        o_ref[pl.ds(src, pt), :] = (y * _silu(g_ref[pl.ds(src, pt), :])).astype(o_ref.dtype)

    for j in range(tc // pt):
        fin(gc_ref, oc_ref, j * pt, j * pt)

    def fin_lat(j, carry):
        fin(gl_ref, ol_ref, pl.multiple_of(j * pt, pt), pl.multiple_of(tc + j * pt, pt))
        return carry

    lax.fori_loop(0, tl // pt, fin_lat, 0)


def _hgrn2(p_c, p_l, lb_logits, norm_g, layer):
    bsz, tc, _ = p_c.shape
    tl = p_l.shape[1]
    t = tc + tl
    depth = lb_logits.shape[0]
    nh = HG_HEADS
    w = HG_DK

    def col(tx, j):
        return pl.BlockSpec((None, tx, w), lambda b, h, j=j: (b, 0, j * nh + h))

    in_specs = ([col(tc, j) for j in range(5)] + [col(tl, j) for j in range(5)]
                + [pl.BlockSpec((depth, 2, w), lambda b, h: (0, 0, h)),
                   pl.BlockSpec((1, w), lambda b, h: (0, 0))])
    out_specs = [pl.BlockSpec((None, tc, w), lambda b, h: (b, 0, h)),
                 pl.BlockSpec((None, tl, w), lambda b, h: (b, 0, h))]
    scratch = [pltpu.VMEM((t, w), F32), pltpu.VMEM((t, w), F32),
               pltpu.VMEM((2, t, w), F32), pltpu.VMEM((2, t, w), F32),
               pltpu.VMEM((2, t, w), BF16), pltpu.VMEM((2, t, w), BF16),
               pltpu.VMEM((2, t, w), F32), pltpu.VMEM((2, t, w), F32),
               pltpu.VMEM((2, w, w), F32)]
    return pl.pallas_call(
        functools.partial(_hgrn2_kernel, layer, tc, tl),
        out_shape=[jax.ShapeDtypeStruct((bsz, tc, nh * w), BF16),
                   jax.ShapeDtypeStruct((bsz, tl, nh * w), BF16)],
        grid=(bsz, nh),
        in_specs=in_specs,
        out_specs=out_specs,
        scratch_shapes=scratch,
        compiler_params=_cparams("parallel", "parallel"),
        name="hgrn2",
    )(p_c, p_c, p_c, p_c, p_c, p_l, p_l, p_l, p_l, p_l, lb_logits, norm_g)


def _rwkv_prep_kernel(nt, p_ref, pp_ref, pn_ref, mu_ref, w0_ref, w2f_ref, w2b_ref, a0_ref, a2_ref, g2_ref,
                      kk_ref, ka_ref, hsum_ref,
                      r_ref, lwf_ref, lwb_ref, k_ref, an_ref, bb_ref, v_ref, g_ref):
    i = pl.program_id(1)
    p = p_ref[...]
    tm = p.shape[0]
    rows = lax.broadcasted_iota(jnp.int32, p.shape, 0)
    first = jnp.where(i > 0, pp_ref[7:8, :], 0.0)
    last = jnp.where(i < nt - 1, pn_ref[0:1, :], 0.0)
    prev = jnp.where(rows == 0, first, pltpu.roll(p, 1, 0))
    nxt = jnp.where(rows == tm - 1, last, pltpu.roll(p, tm - 1, 0))
    xs = p + mu_ref[...] * (0.5 * (prev + nxt) - p)

    c = RW_C
    r = xs[:, 0:c]
    k = xs[:, c:2 * c]
    v = xs[:, 2 * c:3 * c]
    wd = jnp.tanh(xs[:, 3 * c:3 * c + LANES])
    gd = _sigmoid(xs[:, 3 * c + LANES:3 * c + 2 * LANES])
    ad = xs[:, 3 * c + 2 * LANES:3 * c + 3 * LANES]

    a = _sigmoid(a0_ref[...] + _dot_hi(ad, a2_ref[...]))
    kk = k * kk_ref[...]
    nrm = jnp.sqrt(_dot_x3(kk * kk, hsum_ref[...]))
    kk = kk / jnp.maximum(nrm, 1e-12)
    kp = k * (1.0 + (a - 1.0) * ka_ref[...])
    g = _dot_hi(gd, g2_ref[...])

    def log_decay(w2_ref, d):
        wlog = -_softplus(-(w0_ref[d:d + 1, :] + _dot_hi(wd, w2_ref[...]))) - 0.5
        return -jnp.exp(wlog)

    r_ref[...] = r
    lwf_ref[...] = log_decay(w2f_ref, 0)
    lwb_ref[...] = log_decay(w2b_ref, 1)
    k_ref[...] = kp
    an_ref[...] = -kk
    bb_ref[...] = kk * a
    v_ref[...] = v
    g_ref[...] = g


def _rwkv_prep(p, consts, tm):
    bsz, t, n = p.shape
    nt = t // tm
    c = RW_C
    hb = tm // 8
    full = lambda a: pl.BlockSpec(a.shape, lambda b, i, nd=a.ndim: (0,) * nd)
    in_specs = ([pl.BlockSpec((None, tm, n), lambda b, i: (b, i, 0)),
                 pl.BlockSpec((None, 8, n), lambda b, i: (b, jnp.maximum(i * hb - 1, 0), 0)),
                 pl.BlockSpec((None, 8, n), lambda b, i: (b, jnp.minimum((i + 1) * hb, t // 8 - 1), 0))]
                + [full(a) for a in consts])
    out_specs = [pl.BlockSpec((None, tm, c), lambda b, i: (b, i, 0))] * 8
    out_shape = [jax.ShapeDtypeStruct((bsz, t, c), F32)] * 8
    return pl.pallas_call(
        functools.partial(_rwkv_prep_kernel, nt),
        out_shape=out_shape,
        grid=(bsz, nt),
        in_specs=in_specs,
        out_specs=out_specs,
        compiler_params=_cparams("parallel", "parallel"),
        name="rwkv_prep",
    )(p, p, p, *consts)


def _rwkv_scan_kernel(*refs):
    dir_in = (refs[0:6], refs[6:12])
    s0_ref = refs[12]
    y_refs = refs[13:15]
    sout_ref = refs[15]
    s_scr = refs[16]
    nh, hd, c = RW_HEADS, RW_HD, RW_CHUNK
    n = c * nh
    i = pl.program_id(1)
    nch = dir_in[0][0].shape[0] // c

    @pl.when(i == 0)
    def _():
        s_scr[...] = s0_ref[...]

    ti = lax.broadcasted_iota(jnp.int32, (c, c), 0)
    tj = lax.broadcasted_iota(jnp.int32, (c, c), 1)
    tri = (jnp.where(tj <= ti, 1.0, 0.0).astype(BF16), jnp.where(tj >= ti, 1.0, 0.0).astype(BF16))
    tr = lax.broadcasted_iota(jnp.int32, (n, n), 0) // nh
    ts = lax.broadcasted_iota(jnp.int32, (n, n), 1) // nh
    strict = (ts < tr, ts > tr)
    incl = (ts <= tr, ts >= tr)
    rc = lax.broadcasted_iota(jnp.int32, (n, c), 0) // nh
    cc = lax.broadcasted_iota(jnp.int32, (n, c), 1)
    strict_c = (cc < rc, cc > rc)
    hrow = lax.broadcasted_iota(jnp.int32, (nh, nh * hd), 0)
    hcol = lax.broadcasted_iota(jnp.int32, (nh, nh * hd), 1) // hd
    own = jnp.where(hrow == hcol, 1.0, 0.0)

    def chunk(d, c0):
        r_ref, lw_ref, k_ref, a_ref, b_ref, v_ref = dir_in[d]
        sl = pl.ds(c0, c)
        sls = pl.ds(pl.multiple_of(c0 * nh, n), n)
        lw = lw_ref[sl, :]
        g = _dot_m3(tri[d], lw)
        tot = g[c - 1:c, :] if d == 0 else g[0:1, :]
        eg = jnp.exp(g)
        ineg = jnp.exp(-g)
        etg = jnp.exp(tot - g)
        a, b, k = a_ref[sl, :], b_ref[sl, :], k_ref[sl, :]
        x6 = jnp.concatenate([a * jnp.exp(g - lw), r_ref[sl, :] * eg, b * ineg, k * ineg, b * etg, k * etg], axis=0)
        xx = (x6[:, None, :] * own[None, :, :]).reshape(6 * n, nh * hd).astype(BF16)
        lx, rx, bx = xx[0:2 * n], xx[2 * n:4 * n], xx[4 * n:6 * n]
        gm = _dot_nt(lx, rx)
        aab = jnp.where(strict_c[d], _dot_nt(lx[0:n], x6[2 * c:3 * c].astype(BF16)), 0.0)
        aak = jnp.where(strict[d], gm[0:n, n:2 * n], 0.0)
        arb = jnp.where(incl[d], gm[n:2 * n, 0:n], 0.0)
        ark = jnp.where(incl[d], gm[n:2 * n, n:2 * n], 0.0)
        vb = v_ref[sls, :].astype(BF16)
        st = s_scr[d]
        w = _dot_nt(lx, st.astype(BF16))
        rhs = w[0:n] + _dot(aak.astype(BF16), vb)
        blocks = [rhs[nh * t:nh * (t + 1)] for t in range(c)]
        order = range(c) if d == 0 else range(c - 1, -1, -1)
        for s in order:
            later = range(s + 1, c) if d == 0 else range(0, s)
            for t in later:
                blocks[t] = blocks[t] + aab[nh * t:nh * (t + 1), s:s + 1] * blocks[s]
        u = jnp.concatenate(blocks, axis=0)
        uv = jnp.concatenate([u.astype(BF16), vb], axis=0)
        y = w[n:2 * n] + _dot(jnp.concatenate([arb, ark], axis=1).astype(BF16), uv)
        y_refs[d][sls, :] = y
        s_scr[d] = st * jnp.exp(tot) + _dot_tn(uv, bx)

    def step(j, carry):
        chunk(0, pl.multiple_of(j * c, c))
        chunk(1, pl.multiple_of((nch - 1 - j) * c, c))
        return carry

    lax.fori_loop(0, nch, step, 0)

    @pl.when(i == pl.num_programs(1) - 1)
    def _():
        sout_ref[...] = s_scr[...]


def _rwkv_scan(prep, s0, tb):
    r, lwf, lwb, k, an, bb, v = prep[:7]
    bsz, t, ch = r.shape
    nh, hd = RW_HEADS, RW_HD
    nblk = t // tb
    vs = v.reshape(bsz, t * nh, hd)
    fwd = lambda b, i: (b, i, 0)
    bwd = lambda b, i: (b, nblk - 1 - i, 0)
    nat = lambda im: pl.BlockSpec((None, tb, ch), im)
    stk = lambda im: pl.BlockSpec((None, tb * nh, hd), im)
    sspec = pl.BlockSpec((None, 2, hd, ch), lambda b, i: (b, 0, 0, 0))
    yshape = jax.ShapeDtypeStruct((bsz, t * nh, hd), F32)
    yf, yb, s_out = pl.pallas_call(
        _rwkv_scan_kernel,
        out_shape=[yshape, yshape, jax.ShapeDtypeStruct(s0.shape, F32)],
        grid=(bsz, nblk),
        in_specs=[nat(fwd)] * 5 + [stk(fwd)] + [nat(bwd)] * 5 + [stk(bwd)] + [sspec],
        out_specs=[stk(fwd), stk(bwd), sspec],
        scratch_shapes=[pltpu.VMEM((2, hd, ch), F32)],
        compiler_params=_cparams("parallel", "arbitrary"),
        name="rwkv_scan",
    )(r, lwf, k, an, bb, vs, r, lwb, k, an, bb, vs, s0)
    return yf.reshape(bsz, t, ch), yb.reshape(bsz, t, ch), s_out


def _rwkv_fin_kernel(yf_ref, yb_ref, r_ref, k_ref, v_ref, g_ref, rk_ref, lng_ref, lnb_ref, hmean_ref, hsum_ref, o_ref):
    y = yf_ref[...] + yb_ref[...]
    r, k, v = r_ref[...], k_ref[...], v_ref[...]
    mean = _dot_x3(y, hmean_ref[...])
    yc = y - mean
    var = _dot_x3(yc * yc, hmean_ref[...])
    yn = yc * lax.rsqrt(var + RW_LN_EPS) * lng_ref[...] + lnb_ref[...]
    bonus = _dot_x3(r * k * rk_ref[...], hsum_ref[...]) * v
    o_ref[...] = ((yn + bonus) * g_ref[...]).astype(o_ref.dtype)


def _rwkv_fin(yf, yb, prep, rk, lng, lnb, hmean, hsum, tm):
    bsz, t, c = yf.shape
    r, _, _, k, _, _, v, g = prep
    tspec = pl.BlockSpec((None, tm, c), lambda b, i: (b, i, 0))
    vec = pl.BlockSpec((1, c), lambda b, i: (0, 0))
    mat = pl.BlockSpec((c, c), lambda b, i: (0, 0))
    return pl.pallas_call(
        _rwkv_fin_kernel,
        out_shape=jax.ShapeDtypeStruct((bsz, t, c), BF16),
        grid=(bsz, t // tm),
        in_specs=[tspec] * 6 + [vec, vec, vec, mat, mat],
        out_specs=tspec,
        compiler_params=_cparams("parallel", "parallel"),
        name="rwkv_fin",
    )(yf, yb, r, k, v, g, rk, lng, lnb, hmean, hsum)


def _rope(x, cos, sin):
    lane = lax.broadcasted_iota(jnp.int32, x.shape, 1)
    partner = jnp.where((lane % ROPE_AXIS_DIM) < ROPE_AXIS_DIM // 2,
                        pltpu.roll(x, LANES - ROPE_AXIS_DIM // 2, 1), pltpu.roll(x, ROPE_AXIS_DIM // 2, 1))
    return x * cos + partner * sin


def _lambda(lam_ref, lam_init):
    lp = lam_ref[...]
    s1 = jnp.sum(lp[0:1, :] * lp[1:2, :], axis=1, keepdims=True)
    s2 = jnp.sum(lp[2:3, :] * lp[3:4, :], axis=1, keepdims=True)
    return jnp.exp(s1) - jnp.exp(s2) + lam_init


def _diff_attend(q, k, v, lam, sg, lam_init):
    lane = lax.broadcasted_iota(jnp.int32, q.shape, 1)
    qb = q.astype(BF16)
    zero = jnp.zeros_like(qb)
    probs = []
    for m in range(2):
        qm = jnp.where((lane // DA_HD) == m, qb, zero)
        s = _dot_nt(qm, k)
        e = jnp.exp(s - jnp.max(s, axis=-1, keepdims=True))
        probs.append(e / jnp.sum(e, axis=-1, keepdims=True))
    w = probs[0] - lam * probs[1]
    o = _dot(w.astype(BF16), v)
    return _rms(o, DA_SUBLN_EPS) * sg * (1.0 - lam_init)


def _da_lat_kernel(lam_init, tc, tl, q_ref, kc_ref, vc_ref, kl_ref, vl_ref, cos_ref, sin_ref, lam_ref, sg_ref,
                   o_ref, k_s, v_s):
    qi = pl.program_id(2)
    tq = q_ref.shape[0]

    @pl.when(qi == 0)
    def _():
        k_s[0:tc, :] = kc_ref[...].astype(BF16)
        v_s[0:tc, :] = vc_ref[...].astype(BF16)
        k_s[tc:tc + tl, :] = _rope(kl_ref[...], cos_ref[...], sin_ref[...]).astype(BF16)
        v_s[tc:tc + tl, :] = vl_ref[...].astype(BF16)

    r0 = pl.multiple_of(qi * tq, tq)
    q = _rope(q_ref[...], cos_ref[pl.ds(r0, tq), :], sin_ref[pl.ds(r0, tq), :]) * DA_SCALE
    o = _diff_attend(q, k_s[...], v_s[...], _lambda(lam_ref, lam_init), sg_ref[...], lam_init)
    o_ref[...] = o.astype(o_ref.dtype)


def _da_ctx_kernel(lam_init, q_ref, k_ref, v_ref, lam_ref, sg_ref, o_ref):
    o = _diff_attend(q_ref[...] * DA_SCALE, k_ref[...].astype(BF16), v_ref[...].astype(BF16),
                     _lambda(lam_ref, lam_init), sg_ref[...], lam_init)
    o_ref[...] = o.astype(o_ref.dtype)


def _diff_attn(p_c, p_l, cos, sin, lam_p, sg, layer, need_ctx, tq):
    bsz, tc, _ = p_c.shape
    tl = p_l.shape[1]
    nh = DA_HEADS
    w = 2 * DA_HD
    lam_init = 0.8 - 0.6 * math.exp(-0.3 * layer)
    small = [pl.BlockSpec(lam_p.shape, lambda *a: (0, 0)), pl.BlockSpec((1, w), lambda *a: (0, 0))]
    o_l = pl.pallas_call(
        functools.partial(_da_lat_kernel, lam_init, tc, tl),
        out_shape=jax.ShapeDtypeStruct((bsz, tl, nh * w), BF16),
        grid=(bsz, nh, tl // tq),
        in_specs=[pl.BlockSpec((None, tq, w), lambda b, h, i: (b, i, h)),
                  pl.BlockSpec((None, tc, w), lambda b, h, i: (b, 0, nh + h)),
                  pl.BlockSpec((None, tc, w), lambda b, h, i: (b, 0, 2 * nh + h)),
                  pl.BlockSpec((None, tl, w), lambda b, h, i: (b, 0, nh + h)),
                  pl.BlockSpec((None, tl, w), lambda b, h, i: (b, 0, 2 * nh + h)),
                  pl.BlockSpec((tl, w), lambda b, h, i: (0, 0)),
                  pl.BlockSpec((tl, w), lambda b, h, i: (0, 0))] + small,
        out_specs=pl.BlockSpec((None, tq, w), lambda b, h, i: (b, i, h)),
        scratch_shapes=[pltpu.VMEM((tc + tl, w), BF16), pltpu.VMEM((tc + tl, w), BF16)],
        compiler_params=_cparams("parallel", "parallel", "arbitrary"),
        name="diff_attn_lat",
    )(p_l, p_c, p_c, p_l, p_l, cos, sin, lam_p, sg)
    o_c = None
    if need_ctx:
        o_c = pl.pallas_call(
            functools.partial(_da_ctx_kernel, lam_init),
            out_shape=jax.ShapeDtypeStruct((bsz, tc, nh * w), BF16),
            grid=(bsz, nh),
            in_specs=[pl.BlockSpec((None, tc, w), lambda b, h: (b, 0, h)),
                      pl.BlockSpec((None, tc, w), lambda b, h: (b, 0, nh + h)),
                      pl.BlockSpec((None, tc, w), lambda b, h: (b, 0, 2 * nh + h))] + small,
            out_specs=pl.BlockSpec((None, tc, w), lambda b, h: (b, 0, h)),
            compiler_params=_cparams("parallel", "parallel"),
            name="diff_attn_ctx",
        )(p_c, p_c, p_c, lam_p, sg)
    return o_l, o_c


def _rope_tables(tl):
    rows = tl // GRID_W
    t = np.arange(tl)
    pos = np.stack([t // GRID_W, t % GRID_W], axis=1).astype(np.float32)
    inv_freq = (1.0 / (ROPE_BASE ** (jnp.arange(0, ROPE_AXIS_DIM, 2, dtype=F32) / ROPE_AXIS_DIM)))
    lane = np.arange(LANES)
    d = lane % DA_HD
    axis = d // ROPE_AXIS_DIM
    freq = d % (ROPE_AXIS_DIM // 2)
    sign = np.where((d % ROPE_AXIS_DIM) < ROPE_AXIS_DIM // 2, -1.0, 1.0).astype(np.float32)
    ang = jnp.asarray(pos)[:, axis] * inv_freq[freq][None, :]
    del rows
    return jnp.cos(ang), jnp.sin(ang) * sign[None, :]


def _merge_kernel(x_ref, mod_ref, fh_ref, fr_ref, fd_ref, gt_ref, ph_ref, pr_ref, pd_ref, wo_ref, o_ref):
    d = x_ref.shape[-1]
    g = gt_ref[...]
    m = (_sigmoid(g[:, 0:d]) * _dot(fh_ref[...], ph_ref[...])
         + _sigmoid(g[:, d:2 * d]) * _dot(fr_ref[...], pr_ref[...])
         + _sigmoid(g[:, 2 * d:3 * d]) * _dot(fd_ref[...], pd_ref[...]))
    o_ref[...] = x_ref[...] + mod_ref[2:3, :] * _dot(m.astype(BF16), wo_ref[...])


def _merge(x, mod, fh, fr, fd, gate, ph, pr, pd, wo, tm):
    bsz, t, d = x.shape
    c = fh.shape[-1]
    row = lambda n: pl.BlockSpec((None, tm, n), lambda b, i: (b, i, 0))
    wspec = lambda a: pl.BlockSpec(a.shape, lambda b, i: (0, 0))
    return pl.pallas_call(
        _merge_kernel,
        out_shape=jax.ShapeDtypeStruct(x.shape, F32),
        grid=(bsz, t // tm),
        in_specs=[row(d), pl.BlockSpec((None, 6, d), lambda b, i: (b, 0, 0)), row(c), row(c), row(c), row(3 * d),
                  wspec(ph), wspec(pr), wspec(pd), wspec(wo)],
        out_specs=row(d),
        input_output_aliases={0: 0},
        compiler_params=_cparams("parallel", "parallel"),
        name="merge",
    )(x, mod, fh, fr, fd, gate, ph, pr, pd, wo)


def _ffn_prologue(x_ref, g_ref, mod_ref):
    h = _rms(x_ref[...], NORM_EPS) * g_ref[...]
    return h * (1.0 + mod_ref[4:5, :]) + mod_ref[3:4, :]


def _ffn_epilogue(x_ref, mod_ref, acc, fg_ref, o_ref):
    y = x_ref[...] + mod_ref[5:6, :] * acc
    if fg_ref is not None:
        y = _rms(y, NORM_EPS) * fg_ref[...]
    o_ref[...] = y


def _swiglu_step(h, w1_ref, w3_ref, w2_ref):
    a = _dot(h, w1_ref[...])
    u = _silu(a) * _dot(h, w3_ref[...])
    return _dot(u.astype(BF16), w2_ref[...])


def _ffn_kernel(final, x_ref, g_ref, mod_ref, w1_ref, w3_ref, w2_ref, *rest):
    fg_ref, (o_ref, h_s, acc_s) = (rest[0], rest[1:]) if final else (None, rest)
    f = pl.program_id(2)

    @pl.when(f == 0)
    def _():
        h_s[...] = _ffn_prologue(x_ref, g_ref, mod_ref).astype(BF16)
        acc_s[...] = jnp.zeros(acc_s.shape, F32)

    acc_s[...] += _swiglu_step(h_s[...], w1_ref, w3_ref, w2_ref)

    @pl.when(f == pl.num_programs(2) - 1)
    def _():
        _ffn_epilogue(x_ref, mod_ref, acc_s[...], fg_ref, o_ref)


def _ffn(x, g, mod, w1, w3, w2, final_g, tm, tf):
    bsz, t, d = x.shape
    fdim = w1.shape[1]
    final = final_g is not None
    in_specs = [pl.BlockSpec((None, tm, d), lambda b, i, f: (b, i, 0)),
                pl.BlockSpec((1, d), lambda b, i, f: (0, 0)),
                pl.BlockSpec((None, 6, d), lambda b, i, f: (b, 0, 0)),
                pl.BlockSpec((d, tf), lambda b, i, f: (0, f)),
                pl.BlockSpec((d, tf), lambda b, i, f: (0, f)),
                pl.BlockSpec((tf, d), lambda b, i, f: (f, 0))]
    args = [x, g, mod, w1, w3, w2]
    if final:
        in_specs.append(pl.BlockSpec((1, d), lambda b, i, f: (0, 0)))
        args.append(final_g)
    return pl.pallas_call(
        functools.partial(_ffn_kernel, final),
        out_shape=jax.ShapeDtypeStruct(x.shape, F32),
        grid=(bsz, t // tm, fdim // tf),
        in_specs=in_specs,
        out_specs=pl.BlockSpec((None, tm, d), lambda b, i, f: (b, i, 0)),
        scratch_shapes=[pltpu.VMEM((tm, d), BF16), pltpu.VMEM((tm, d), F32)],
        compiler_params=_cparams("parallel", "parallel", "arbitrary"),
        name="ffn",
    )(*args)


def _moe_kernel(final, x_ref, g_ref, mod_ref, rt_ref, w1_ref, w3_ref, w2_ref, *rest):
    fg_ref, (o_ref, h_s, acc_s, cmb_s) = (rest[0], rest[1:]) if final else (None, rest)
    e = pl.program_id(2)
    f = pl.program_id(3)
    lane = lax.broadcasted_iota(jnp.int32, cmb_s.shape, 1).astype(F32)

    @pl.when((e == 0) & (f == 0))
    def _():
        h = _ffn_prologue(x_ref, g_ref, mod_ref)
        h_s[...] = h.astype(BF16)
        acc_s[...] = jnp.zeros(acc_s.shape, F32)
        logits = jnp.where(lane < N_EXPERTS, _dot_hi(h, rt_ref[...]), -jnp.inf)
        m1 = jnp.max(logits, axis=-1, keepdims=True)
        i1 = jnp.min(jnp.where(logits == m1, lane, float(LANES)), axis=-1, keepdims=True)
        rest_l = jnp.where(lane == i1, -jnp.inf, logits)
        m2 = jnp.max(rest_l, axis=-1, keepdims=True)
        i2 = jnp.min(jnp.where(rest_l == m2, lane, float(LANES)), axis=-1, keepdims=True)
        e2 = jnp.exp(m2 - m1)
        cmb_s[...] = jnp.where(lane == i1, 1.0 / (1.0 + e2), 0.0) + jnp.where(lane == i2, e2 / (1.0 + e2), 0.0)

    ce = jnp.sum(jnp.where(lane == e.astype(F32), cmb_s[...], 0.0), axis=-1, keepdims=True)
    acc_s[...] += ce * _swiglu_step(h_s[...], w1_ref, w3_ref, w2_ref)

    @pl.when((e == pl.num_programs(2) - 1) & (f == pl.num_programs(3) - 1))
    def _():
        _ffn_epilogue(x_ref, mod_ref, acc_s[...], fg_ref, o_ref)


def _moe(x, g, mod, router, w1, w3, w2, final_g, tm, tf):
    bsz, t, d = x.shape
    ne, _, fdim = w1.shape
    final = final_g is not None
    in_specs = [pl.BlockSpec((None, tm, d), lambda b, i, e, f: (b, i, 0)),
                pl.BlockSpec((1, d), lambda b, i, e, f: (0, 0)),
                pl.BlockSpec((None, 6, d), lambda b, i, e, f: (b, 0, 0)),
                pl.BlockSpec((d, LANES), lambda b, i, e, f: (0, 0)),
                pl.BlockSpec((None, d, tf), lambda b, i, e, f: (e, 0, f)),
                pl.BlockSpec((None, d, tf), lambda b, i, e, f: (e, 0, f)),
                pl.BlockSpec((None, tf, d), lambda b, i, e, f: (e, f, 0))]
    args = [x, g, mod, router, w1, w3, w2]
    if final:
        in_specs.append(pl.BlockSpec((1, d), lambda b, i, e, f: (0, 0)))
        args.append(final_g)
    return pl.pallas_call(
        functools.partial(_moe_kernel, final),
        out_shape=jax.ShapeDtypeStruct(x.shape, F32),
        grid=(bsz, t // tm, ne, fdim // tf),
        in_specs=in_specs,
        out_specs=pl.BlockSpec((None, tm, d), lambda b, i, e, f: (b, i, 0)),
        scratch_shapes=[pltpu.VMEM((tm, d), BF16), pltpu.VMEM((tm, d), F32), pltpu.VMEM((tm, LANES), F32)],
        compiler_params=_cparams("parallel", "parallel", "arbitrary", "arbitrary"),
        name="moe",
    )(*args)


def _row_tile(t, pref):
    tm = min(pref, t)
    while t % tm:
        tm //= 2
    return tm


def kernel(x, c, ctx, c_ctx, ada_w, ada_b, norm_mix_g, norm_ffn_g, final_norm_g, w_in, hg_lb_logits, hg_norm_g,
           hg_proj, rw_mu, rw_w0, rw_w2, rw_a0, rw_a2, rw_g2, rw_k_k, rw_k_a, rw_r_k, rw_ln_g, rw_ln_b, rw_proj,
           da_lambda, da_subln_g, da_proj, w_out, ffn_w1, ffn_w3, ffn_w2, moe_router, moe_w1, moe_w3, moe_w2):
    bsz, tl, d = x.shape
    tc = ctx.shape[1]
    depth = ada_w.shape[0]
    hg_cols = 5 * HG_HEADS * HG_DK
    rw_cols = 3 * RW_C + 2 * RW_W_RANK + RW_A_RANK + RW_G_RANK
    da_cols = 3 * DA_HEADS * 2 * DA_HD
    o_rw = hg_cols
    o_da = o_rw + rw_cols
    o_gt = o_da + da_cols

    ad_lo = 3 * RW_C + 2 * RW_W_RANK
    ad_hi = ad_lo + RW_A_RANK

    def rw_reorder(a):
        pad = jnp.zeros(a.shape[:-1] + ((-rw_cols) % LANES,), a.dtype)
        return jnp.concatenate([a[..., :ad_lo], a[..., ad_hi:], a[..., ad_lo:ad_hi], pad], axis=-1)

    cvec = jnp.zeros((16, d), F32).at[:bsz].set(c).at[bsz].set(c_ctx)
    cos, sin = _rope_tables(tl)
    head_id = np.arange(RW_C) // RW_HD
    hsum = jnp.asarray((head_id[:, None] == head_id[None, :]).astype(np.float32), BF16)
    hmean = (hsum.astype(F32) / RW_HD).astype(BF16)
    zpad = lambda a, rows: jnp.concatenate([a, jnp.zeros((rows - a.shape[0],) + a.shape[1:], a.dtype)], axis=0)

    tm_l = _row_tile(tl, 512)
    tm_c = _row_tile(tc, 256)
    xl, xc = x, ctx
    for l in range(depth):
        need_ctx = l < depth - 1
        last = l == depth - 1
        mods = _adaln(cvec, ada_w[l], ada_b[l])
        mod_l = mods[:bsz].reshape(bsz, 6, d)
        mod_c = jnp.broadcast_to(mods[bsz].reshape(1, 6, d), (bsz, 6, d))

        wl = w_in[l]
        w_hg = wl[:, :o_rw].astype(BF16)
        w_rw = rw_reorder(wl[:, o_rw:o_da]).astype(BF16)
        w_da = wl[:, o_da:o_gt].astype(BF16)
        w_gt = wl[:, o_gt:].astype(BF16)
        gmix = norm_mix_g[l].reshape(1, d)
        proj = lambda xx, mod, w, tm: _inproj(xx, gmix, mod, w, tm)

        fh_c, fh_l = _hgrn2(proj(xc, mod_c, w_hg, tm_c), proj(xl, mod_l, w_hg, tm_l),
                            hg_lb_logits, hg_norm_g[l].reshape(1, HG_DK), l)

        mu = rw_reorder(rw_mu[l]).reshape(1, -1)
        consts = (mu, rw_w0[l], zpad(rw_w2[l, 0], LANES),
                  jnp.concatenate([jnp.zeros_like(rw_w2[l, 1]), rw_w2[l, 1]], axis=0),
                  rw_a0[l].reshape(1, RW_C), zpad(rw_a2[l], LANES), rw_g2[l],
                  rw_k_k[l].reshape(1, RW_C), rw_k_a[l].reshape(1, RW_C), hsum)
        prep_c = _rwkv_prep(proj(xc, mod_c, w_rw, tm_c), consts, tm_c)
        prep_l = _rwkv_prep(proj(xl, mod_l, w_rw, tm_l), consts, _row_tile(tl, 256))
        s0 = jnp.zeros((bsz, 2, RW_HD, RW_C), F32)
        yc_f, yc_b, s_ctx = _rwkv_scan(prep_c, s0, _row_tile(tc, 256))
        yl_f, yl_b, _ = _rwkv_scan(prep_l, s_ctx, _row_tile(tl, 256))
        fin = lambda yf, yb, prep, tm: _rwkv_fin(yf, yb, prep, rw_r_k[l].reshape(1, RW_C),
                                                 rw_ln_g[l].reshape(1, RW_C), rw_ln_b[l].reshape(1, RW_C),
                                                 hmean, hsum, tm)
        fr_l = fin(yl_f, yl_b, prep_l, _row_tile(tl, 256))

        pda_c = proj(xc, mod_c, w_da, tm_c)
        pda_l = proj(xl, mod_l, w_da, tm_l)
        fd_l, fd_c = _diff_attn(pda_c, pda_l, cos, sin, da_lambda[l], da_subln_g[l].reshape(1, -1), l, need_ctx,
                                _row_tile(tl, 256))

        projs = (hg_proj[l].astype(BF16), rw_proj[l].astype(BF16), da_proj[l].astype(BF16), w_out[l].astype(BF16))
        xl = _merge(xl, mod_l, fh_l, fr_l, fd_l, proj(xl, mod_l, w_gt, tm_l), *projs, tm_l)
        if need_ctx:
            fr_c = fin(yc_f, yc_b, prep_c, tm_c)
            xc = _merge(xc, mod_c, fh_c, fr_c, fd_c, proj(xc, mod_c, w_gt, tm_c), *projs, tm_c)

        gffn = norm_ffn_g[l].reshape(1, d)
        fg = final_norm_g.reshape(1, d) if last else None
        j = l // 2
        if l % 2 == 0:
            w1, w3, w2 = ffn_w1[j].astype(BF16), ffn_w3[j].astype(BF16), ffn_w2[j].astype(BF16)
            tf = w1.shape[1] // 2
            xl = _ffn(xl, gffn, mod_l, w1, w3, w2, fg, tm_l, tf)
            if need_ctx:
                xc = _ffn(xc, gffn, mod_c, w1, w3, w2, None, tm_c, tf)
        else:
            w1, w3, w2 = moe_w1[j].astype(BF16), moe_w3[j].astype(BF16), moe_w2[j].astype(BF16)
            tf = w1.shape[2] // 2
            router = jnp.pad(moe_router[j], ((0, 0), (0, LANES - N_EXPERTS)))
            xl = _moe(xl, gffn, mod_l, router, w1, w3, w2, fg, tm_l, tf)
            if need_ctx:
                xc = _moe(xc, gffn, mod_c, router, w1, w3, w2, None, tm_c, tf)
    if depth == 0:
        raise ValueError("depth must be positive")
    return xl
```

```python
import functools
import math

import numpy as np
import jax
import jax.numpy as jnp
from jax import lax
from jax.experimental import pallas as pl
from jax.experimental.pallas import tpu as pltpu

F32 = jnp.float32
BF16 = jnp.bfloat16

GRID_W = 64
HG_HEADS = 4
HG_DK = 128
RW_HEADS = 8
RW_HD = 64
RW_C = RW_HEADS * RW_HD
RW_W_RANK = 64
RW_A_RANK = 64
RW_G_RANK = 128
RW_LN_EPS = 64e-5
DA_HEADS = 4
DA_HD = 64
DA_SUBLN_EPS = 1e-5
DA_SCALE = DA_HD ** -0.5
ROPE_AXIS_DIM = DA_HD // 2
ROPE_BASE = 10000.0
N_EXPERTS = 8
TOP_K = 2
NORM_EPS = 1e-6

LANES = 128
PREP_TILE = 256
RW_CHUNK = 16
VMEM_LIMIT = 52 * 1024 * 1024


def _cparams(*sem):
    return pltpu.CompilerParams(dimension_semantics=sem, vmem_limit_bytes=VMEM_LIMIT)


def _dot(a, b):
    return jnp.dot(a, b, preferred_element_type=F32)


def _dot_nt(a, b):
    return lax.dot_general(a, b, (((1,), (1,)), ((), ())), preferred_element_type=F32)


def _dot_tn(a, b):
    return lax.dot_general(a, b, (((0,), (0,)), ((), ())), preferred_element_type=F32)


def _split2(x):
    hi = x.astype(BF16)
    lo = (x - hi.astype(F32)).astype(BF16)
    return hi, lo


def _split3(x):
    hi = x.astype(BF16)
    r = x - hi.astype(F32)
    mid = r.astype(BF16)
    lo = (r - mid.astype(F32)).astype(BF16)
    return hi, mid, lo


def _dot_hi(a, b):
    ah, al = _split2(a)
    bh, bl = _split2(b)
    return _dot(ah, bh) + (_dot(ah, bl) + _dot(al, bh))


def _dot_x3(x, m):
    hi, mid, lo = _split3(x)
    return _dot(hi, m) + (_dot(mid, m) + _dot(lo, m))


def _dot_m3(m, x):
    hi, mid, lo = _split3(x)
    return _dot(m, hi) + (_dot(m, mid) + _dot(m, lo))


def _sigmoid(x):
    return jax.nn.sigmoid(x)


def _silu(x):
    return x * jax.nn.sigmoid(x)


def _log_sigmoid(z):
    return jnp.minimum(z, 0.0) - jnp.log1p(jnp.exp(-jnp.abs(z)))


def _softplus(x):
    return jnp.maximum(x, 0.0) + jnp.log1p(jnp.exp(-jnp.abs(x)))


def _rms(x, eps):
    return x * lax.rsqrt(jnp.mean(x * x, axis=-1, keepdims=True) + eps)


def _adaln_kernel(c_ref, w_ref, b_ref, o_ref):
    o_ref[...] = _dot_hi(_silu(c_ref[...]), w_ref[...]) + b_ref[...]


def _adaln(cvec, w, b):
    rows, d = cvec.shape
    n = w.shape[1]
    tn = n // 4
    return pl.pallas_call(
        _adaln_kernel,
        out_shape=jax.ShapeDtypeStruct((rows, n), F32),
        grid=(n // tn,),
        in_specs=[pl.BlockSpec((rows, d), lambda j: (0, 0)),
                  pl.BlockSpec((d, tn), lambda j: (0, j)),
                  pl.BlockSpec((1, tn), lambda j: (0, j))],
        out_specs=pl.BlockSpec((rows, tn), lambda j: (0, j)),
        compiler_params=_cparams("arbitrary"),
        name="adaln",
    )(cvec, w, b.reshape(1, n))


def _inproj_kernel(x_ref, g_ref, mod_ref, w_ref, o_ref):
    h = _rms(x_ref[...], NORM_EPS) * g_ref[...]
    h = h * (1.0 + mod_ref[1:2, :]) + mod_ref[0:1, :]
    o_ref[...] = _dot(h.astype(BF16), w_ref[...])


def _inproj(x, g, mod, w, tm):
    bsz, t, d = x.shape
    n = w.shape[1]
    return pl.pallas_call(
        _inproj_kernel,
        out_shape=jax.ShapeDtypeStruct((bsz, t, n), F32),
        grid=(bsz, t // tm),
        in_specs=[pl.BlockSpec((None, tm, d), lambda b, i: (b, i, 0)),
                  pl.BlockSpec((1, d), lambda b, i: (0, 0)),
                  pl.BlockSpec((None, 6, d), lambda b, i: (b, 0, 0)),
                  pl.BlockSpec((d, n), lambda b, i: (0, 0))],
        out_specs=pl.BlockSpec((None, tm, n), lambda b, i: (b, i, 0)),
        compiler_params=_cparams("parallel", "parallel"),
        name="inproj",
    )(x, g, mod, w)


def _hgrn2_kernel(layer, tc, tl,
                  qc_ref, ffc_ref, fbc_ref, ic_ref, gc_ref,
                  ql_ref, ffl_ref, fbl_ref, il_ref, gl_ref,
                  lbl_ref, ng_ref, msk_ref,
                  oc_ref, ol_ref,
                  v_s, qe_s, ke_s, et_s, o_s, st_s):
    nct = tc // PREP_TILE
    nlt = tl // PREP_TILE
    pt = PREP_TILE
    levels = [2 ** j for j in range(1, pt.bit_length())]

    lg = lbl_ref[...]
    mx = jnp.max(lg, axis=0, keepdims=True)
    ex = jnp.exp(lg - mx)
    pr = ex / jnp.sum(ex, axis=0, keepdims=True)
    lb = jnp.zeros(lg.shape[1:], F32)
    for j in range(1, layer + 1):
        lb = lb + pr[j]

    ri = lax.broadcasted_iota(jnp.int32, (pt, pt), 0)
    ci = lax.broadcasted_iota(jnp.int32, (pt, pt), 1)
    tri = (jnp.where(ci <= ri, 1.0, 0.0).astype(BF16), jnp.where(ci >= ri, 1.0, 0.0).astype(BF16))
    sub = lax.broadcasted_iota(jnp.int32, (pt // 8, 8, LANES), 1)

    def midpoint(c, g, d):
        r = g // 2 - 1 + d
        if g >= 8:
            c3 = c.reshape(pt // g, g, LANES)
            return jnp.broadcast_to(c3[:, r:r + 1, :], c3.shape).reshape(pt, LANES)
        c3 = c.reshape(pt // 8, 8, LANES)
        m = jnp.broadcast_to(c3[:, r:r + 1, :], c3.shape)
        for j in range(1, 8 // g):
            m = jnp.where(sub >= j * g, jnp.broadcast_to(c3[:, j * g + r:j * g + r + 1, :], c3.shape), m)
        return m.reshape(pt, LANES)

    def prep(q_ref, ff_ref, fb_ref, i_ref, src, dst, tile):
        q = _silu(q_ref[pl.ds(src, pt), :])
        v = i_ref[pl.ds(src, pt), :]
        vb = v.astype(BF16)
        v_s[pl.ds(dst, pt), :] = vb
        for d, z_ref in enumerate((ff_ref, fb_ref)):
            z = z_ref[pl.ds(src, pt), :]
            lbd = lb[d:d + 1, :]
            a = jnp.log(lbd)
            b = jnp.log1p(-lbd) + _log_sigmoid(z)
            lf = jnp.maximum(a, b) + jnp.log1p(jnp.exp(-jnp.abs(a - b)))
            k = (1.0 - lbd) * _sigmoid(-z)
            c = _dot_m3(tri[d], lf)
            tot = c[pt - 1:pt, :] if d == 0 else c[0:1, :]
            att = jnp.zeros((pt, pt), F32)
            for lvl, g in enumerate(levels):
                e = jnp.exp(-jnp.abs(c - midpoint(c, g, d)))
                att = att + _dot_nt((q * e).astype(BF16), (k * e).astype(BF16)) * msk_ref[d, lvl]
            o = _dot(att.astype(BF16), vb) + jnp.sum(q * k, axis=1, keepdims=True) * v
            o_s[d, pl.ds(dst, pt), :] = o
            qe_s[d, pl.ds(dst, pt), :] = (q * jnp.exp(c)).astype(BF16)
            ke_s[d, pl.ds(dst, pt), :] = (k * jnp.exp(tot - c)).astype(BF16)
            et_s[d, pl.ds(pl.multiple_of(tile * 8, 8), 8), :] = jnp.broadcast_to(jnp.exp(tot), (8, LANES))

    for j in range(nct):
        prep(qc_ref, ffc_ref, fbc_ref, ic_ref, j * pt, j * pt, j)

    def prep_lat(j, carry):
        src = pl.multiple_of(j * pt, pt)
        prep(ql_ref, ffl_ref, fbl_ref, il_ref, src, pl.multiple_of(tc + j * pt, pt), nct + j)
        return carry

    lax.fori_loop(0, nlt, prep_lat, 0)

    st_s[...] = jnp.zeros(st_s.shape, F32)

    def carry_state(d, tile):
        sl = pl.ds(pl.multiple_of(tile * pt, pt), pt)
        st = st_s[d]
        o_s[d, sl, :] = o_s[d, sl, :] + _dot_nt(qe_s[d, sl, :], st.astype(BF16))
        upd = _dot_tn(v_s[sl, :], ke_s[d, sl, :])
        st_s[d] = st * et_s[d, pl.ds(pl.multiple_of(tile * 8, 8), 1), :] + upd

    def step(i, carry):
        carry_state(0, i)
        carry_state(1, jnp.where(i < nct, nct - 1 - i, nct + nlt - 1 - (i - nct)))
        return carry

    lax.fori_loop(0, nct + nlt, step, 0)

    ng = ng_ref[...]

    def fin(g_ref, o_ref, src, dst):
        o = o_s[0, pl.ds(dst, pt), :] + o_s[1, pl.ds(dst, pt), :]
        y = _rms(o, NORM_EPS) * ng
        o_ref[pl.ds(src, pt), :] = (y * _silu(g_ref[pl.ds(src, pt), :])).astype(o_ref.dtype)

    for j in range(tc // pt):
        fin(gc_ref, oc_ref, j * pt, j * pt)

    def fin_lat(j, carry):
        fin(gl_ref, ol_ref, pl.multiple_of(j * pt, pt), pl.multiple_of(tc + j * pt, pt))
        return carry

    lax.fori_loop(0, tl // pt, fin_lat, 0)


def _gla_level_masks():
    pt = PREP_TILE
    t = np.arange(pt)[:, None]
    s = np.arange(pt)[None, :]
    out = np.zeros((2, pt.bit_length() - 1, pt, pt), np.float32)
    for lvl in range(out.shape[1]):
        g = 2 << lvl
        same = (t // g) == (s // g)
        out[0, lvl] = same & (t % g >= g // 2) & (s % g < g // 2)
        out[1, lvl] = same & (t % g < g // 2) & (s % g >= g // 2)
    return jnp.asarray(out)


def _hgrn2(p_c, p_l, lb_logits, norm_g, layer):
    bsz, tc, _ = p_c.shape
    tl = p_l.shape[1]
    t = tc + tl
    depth = lb_logits.shape[0]
    nh = HG_HEADS
    w = HG_DK

    def col(tx, j):
        return pl.BlockSpec((None, tx, w), lambda b, h, j=j: (b, 0, j * nh + h))

    masks = _gla_level_masks()
    in_specs = ([col(tc, j) for j in range(5)] + [col(tl, j) for j in range(5)]
                + [pl.BlockSpec((depth, 2, w), lambda b, h: (0, 0, h)),
                   pl.BlockSpec((1, w), lambda b, h: (0, 0)),
                   pl.BlockSpec(masks.shape, lambda b, h: (0, 0, 0, 0))])
    out_specs = [pl.BlockSpec((None, tc, w), lambda b, h: (b, 0, h)),
                 pl.BlockSpec((None, tl, w), lambda b, h: (b, 0, h))]
    scratch = [pltpu.VMEM((t, w), BF16),
               pltpu.VMEM((2, t, w), BF16), pltpu.VMEM((2, t, w), BF16),
               pltpu.VMEM((2, 8 * (t // PREP_TILE), w), F32), pltpu.VMEM((2, t, w), F32),
               pltpu.VMEM((2, w, w), F32)]
    return pl.pallas_call(
        functools.partial(_hgrn2_kernel, layer, tc, tl),
        out_shape=[jax.ShapeDtypeStruct((bsz, tc, nh * w), BF16),
                   jax.ShapeDtypeStruct((bsz, tl, nh * w), BF16)],
        grid=(bsz, nh),
        in_specs=in_specs,
        out_specs=out_specs,
        scratch_shapes=scratch,
        compiler_params=_cparams("parallel", "parallel"),
        name="hgrn2",
    )(p_c, p_c, p_c, p_c, p_c, p_l, p_l, p_l, p_l, p_l, lb_logits, norm_g, masks)


def _rwkv_prep_kernel(nt, p_ref, pp_ref, pn_ref, mu_ref, w0_ref, w2f_ref, w2b_ref, a0_ref, a2_ref, g2_ref,
                      kk_ref, ka_ref, hsum_ref,
                      r_ref, lwf_ref, lwb_ref, k_ref, an_ref, bb_ref, v_ref, g_ref):
    i = pl.program_id(1)
    p = p_ref[...]
    tm = p.shape[0]
    rows = lax.broadcasted_iota(jnp.int32, p.shape, 0)
    first = jnp.where(i > 0, pp_ref[7:8, :], 0.0)
    last = jnp.where(i < nt - 1, pn_ref[0:1, :], 0.0)
    prev = jnp.where(rows == 0, first, pltpu.roll(p, 1, 0))
    nxt = jnp.where(rows == tm - 1, last, pltpu.roll(p, tm - 1, 0))
    xs = p + mu_ref[...] * (0.5 * (prev + nxt) - p)

    c = RW_C
    r = xs[:, 0:c]
    k = xs[:, c:2 * c]
    v = xs[:, 2 * c:3 * c]
    wd = jnp.tanh(xs[:, 3 * c:3 * c + LANES])
    gd = _sigmoid(xs[:, 3 * c + LANES:3 * c + 2 * LANES])
    ad = xs[:, 3 * c + 2 * LANES:3 * c + 3 * LANES]

    a = _sigmoid(a0_ref[...] + _dot_hi(ad, a2_ref[...]))
    kk = k * kk_ref[...]
    nrm = jnp.sqrt(_dot_x3(kk * kk, hsum_ref[...]))
    kk = kk / jnp.maximum(nrm, 1e-12)
    kp = k * (1.0 + (a - 1.0) * ka_ref[...])
    g = _dot_hi(gd, g2_ref[...])

    def log_decay(w2_ref, d):
        wlog = -_softplus(-(w0_ref[d:d + 1, :] + _dot_hi(wd, w2_ref[...]))) - 0.5
        return -jnp.exp(wlog)

    r_ref[...] = r
    lwf_ref[...] = log_decay(w2f_ref, 0)
    lwb_ref[...] = log_decay(w2b_ref, 1)
    k_ref[...] = kp
    an_ref[...] = -kk
    bb_ref[...] = kk * a
    v_ref[...] = v
    g_ref[...] = g


def _rwkv_prep(p, consts, tm):
    bsz, t, n = p.shape
    nt = t // tm
    c = RW_C
    hb = tm // 8
    full = lambda a: pl.BlockSpec(a.shape, lambda b, i, nd=a.ndim: (0,) * nd)
    in_specs = ([pl.BlockSpec((None, tm, n), lambda b, i: (b, i, 0)),
                 pl.BlockSpec((None, 8, n), lambda b, i: (b, jnp.maximum(i * hb - 1, 0), 0)),
                 pl.BlockSpec((None, 8, n), lambda b, i: (b, jnp.minimum((i + 1) * hb, t // 8 - 1), 0))]
                + [full(a) for a in consts])
    out_specs = [pl.BlockSpec((None, tm, c), lambda b, i: (b, i, 0))] * 8
    out_shape = [jax.ShapeDtypeStruct((bsz, t, c), F32)] * 8
    return pl.pallas_call(
        functools.partial(_rwkv_prep_kernel, nt),
        out_shape=out_shape,
        grid=(bsz, nt),
        in_specs=in_specs,
        out_specs=out_specs,
        compiler_params=_cparams("parallel", "parallel"),
        name="rwkv_prep",
    )(p, p, p, *consts)


def _rwkv_scan_kernel(*refs):
    dir_in = (refs[0:6], refs[6:12])
    s0_ref = refs[12]
    y_refs = refs[13:15]
    sout_ref = refs[15]
    s_scr = refs[16]
    nh, hd, c = RW_HEADS, RW_HD, RW_CHUNK
    n = c * nh
    i = pl.program_id(1)
    nch = dir_in[0][0].shape[0] // c

    @pl.when(i == 0)
    def _():
        s_scr[...] = s0_ref[...]

    ti = lax.broadcasted_iota(jnp.int32, (c, c), 0)
    tj = lax.broadcasted_iota(jnp.int32, (c, c), 1)
    tri = (jnp.where(tj <= ti, 1.0, 0.0).astype(BF16), jnp.where(tj >= ti, 1.0, 0.0).astype(BF16))
    tr = lax.broadcasted_iota(jnp.int32, (n, n), 0) // nh
    ts = lax.broadcasted_iota(jnp.int32, (n, n), 1) // nh
    strict = (ts < tr, ts > tr)
    incl = (ts <= tr, ts >= tr)
    rc = lax.broadcasted_iota(jnp.int32, (n, c), 0) // nh
    cc = lax.broadcasted_iota(jnp.int32, (n, c), 1)
    strict_c = (cc < rc, cc > rc)
    hrow = lax.broadcasted_iota(jnp.int32, (nh, nh * hd), 0)
    hcol = lax.broadcasted_iota(jnp.int32, (nh, nh * hd), 1) // hd
    own = jnp.where(hrow == hcol, 1.0, 0.0)

    def chunk(d, c0):
        r_ref, lw_ref, k_ref, a_ref, b_ref, v_ref = dir_in[d]
        sl = pl.ds(c0, c)
        sls = pl.ds(pl.multiple_of(c0 * nh, n), n)
        lw = lw_ref[sl, :]
        g = _dot_m3(tri[d], lw)
        tot = g[c - 1:c, :] if d == 0 else g[0:1, :]
        eg = jnp.exp(g)
        ineg = jnp.exp(-g)
        etg = jnp.exp(tot - g)
        a, b, k = a_ref[sl, :], b_ref[sl, :], k_ref[sl, :]
        x6 = jnp.concatenate([a * jnp.exp(g - lw), r_ref[sl, :] * eg, b * ineg, k * ineg, b * etg, k * etg], axis=0)
        x6b = x6.astype(BF16)
        xx = (x6[:, None, :] * own[None, :, :]).reshape(6 * n, nh * hd).astype(BF16)
        lx, rx, bx = xx[0:2 * n], xx[2 * n:4 * n], xx[4 * n:6 * n]
        gm = _dot_nt(lx, rx)
        aab = jnp.where(strict_c[d], _dot_nt(lx[0:n], x6b[2 * c:3 * c]), 0.0)
        aak = jnp.where(strict[d], gm[0:n, n:2 * n], 0.0)
        arb = jnp.where(incl[d], gm[n:2 * n, 0:n], 0.0)
        ark = jnp.where(incl[d], gm[n:2 * n, n:2 * n], 0.0)
        vb = v_ref[sls, :].astype(BF16)
        st = s_scr[d]
        w = _dot_nt(lx, st.astype(BF16))
        rhs = w[0:n] + _dot(aak.astype(BF16), vb)
        blocks = [rhs[nh * t:nh * (t + 1)] for t in range(c)]
        order = range(c) if d == 0 else range(c - 1, -1, -1)
        for s in order:
            later = range(s + 1, c) if d == 0 else range(0, s)
            for t in later:
                blocks[t] = blocks[t] + aab[nh * t:nh * (t + 1), s:s + 1] * blocks[s]
        u = jnp.concatenate(blocks, axis=0)
        uv = jnp.concatenate([u.astype(BF16), vb], axis=0)
        y = w[n:2 * n] + _dot(jnp.concatenate([arb, ark], axis=1).astype(BF16), uv)
        y_refs[d][sls, :] = y
        s_scr[d] = st * jnp.exp(tot) + _dot_tn(uv, bx)

    def step(j, carry):
        chunk(0, pl.multiple_of(j * c, c))
        chunk(1, pl.multiple_of((nch - 1 - j) * c, c))
        return carry

    lax.fori_loop(0, nch, step, 0)

    @pl.when(i == pl.num_programs(1) - 1)
    def _():
        sout_ref[...] = s_scr[...]


def _rwkv_scan(prep, s0, tb):
    r, lwf, lwb, k, an, bb, v = prep[:7]
    bsz, t, ch = r.shape
    nh, hd = RW_HEADS, RW_HD
    nblk = t // tb
    vs = v.reshape(bsz, t * nh, hd)
    fwd = lambda b, i: (b, i, 0)
    bwd = lambda b, i: (b, nblk - 1 - i, 0)
    nat = lambda im: pl.BlockSpec((None, tb, ch), im)
    stk = lambda im: pl.BlockSpec((None, tb * nh, hd), im)
    sspec = pl.BlockSpec((None, 2, hd, ch), lambda b, i: (b, 0, 0, 0))
    yshape = jax.ShapeDtypeStruct((bsz, t * nh, hd), F32)
    yf, yb, s_out = pl.pallas_call(
        _rwkv_scan_kernel,
        out_shape=[yshape, yshape, jax.ShapeDtypeStruct(s0.shape, F32)],
        grid=(bsz, nblk),
        in_specs=[nat(fwd)] * 5 + [stk(fwd)] + [nat(bwd)] * 5 + [stk(bwd)] + [sspec],
        out_specs=[stk(fwd), stk(bwd), sspec],
        scratch_shapes=[pltpu.VMEM((2, hd, ch), F32)],
        compiler_params=_cparams("parallel", "arbitrary"),
        name="rwkv_scan",
    )(r, lwf, k, an, bb, vs, r, lwb, k, an, bb, vs, s0)
    return yf.reshape(bsz, t, ch), yb.reshape(bsz, t, ch), s_out


def _rwkv_fin_kernel(yf_ref, yb_ref, r_ref, k_ref, v_ref, g_ref, rk_ref, lng_ref, lnb_ref, hmean_ref, hsum_ref, o_ref):
    y = yf_ref[...] + yb_ref[...]
    r, k, v = r_ref[...], k_ref[...], v_ref[...]
    mean = _dot_x3(y, hmean_ref[...])
    yc = y - mean
    var = _dot_x3(yc * yc, hmean_ref[...])
    yn = yc * lax.rsqrt(var + RW_LN_EPS) * lng_ref[...] + lnb_ref[...]
    bonus = _dot_x3(r * k * rk_ref[...], hsum_ref[...]) * v
    o_ref[...] = ((yn + bonus) * g_ref[...]).astype(o_ref.dtype)


def _rwkv_fin(yf, yb, prep, rk, lng, lnb, hmean, hsum, tm):
    bsz, t, c = yf.shape
    r, _, _, k, _, _, v, g = prep
    tspec = pl.BlockSpec((None, tm, c), lambda b, i: (b, i, 0))
    vec = pl.BlockSpec((1, c), lambda b, i: (0, 0))
    mat = pl.BlockSpec((c, c), lambda b, i: (0, 0))
    return pl.pallas_call(
        _rwkv_fin_kernel,
        out_shape=jax.ShapeDtypeStruct((bsz, t, c), BF16),
        grid=(bsz, t // tm),
        in_specs=[tspec] * 6 + [vec, vec, vec, mat, mat],
        out_specs=tspec,
        compiler_params=_cparams("parallel", "parallel"),
        name="rwkv_fin",
    )(yf, yb, r, k, v, g, rk, lng, lnb, hmean, hsum)


def _rope(x, cos, sin):
    lane = lax.broadcasted_iota(jnp.int32, x.shape, 1)
    partner = jnp.where((lane % ROPE_AXIS_DIM) < ROPE_AXIS_DIM // 2,
                        pltpu.roll(x, LANES - ROPE_AXIS_DIM // 2, 1), pltpu.roll(x, ROPE_AXIS_DIM // 2, 1))
    return x * cos + partner * sin


def _lambda(lam_ref, lam_init):
    lp = lam_ref[...]
    s1 = jnp.sum(lp[0:1, :] * lp[1:2, :], axis=1, keepdims=True)
    s2 = jnp.sum(lp[2:3, :] * lp[3:4, :], axis=1, keepdims=True)
    return jnp.exp(s1) - jnp.exp(s2) + lam_init


def _diff_attend(q, k, v, lam, sg, lam_init):
    lane = lax.broadcasted_iota(jnp.int32, q.shape, 1)
    qb = q.astype(BF16)
    zero = jnp.zeros_like(qb)
    probs = []
    for m in range(2):
        qm = jnp.where((lane // DA_HD) == m, qb, zero)
        s = _dot_nt(qm, k)
        e = jnp.exp(s - jnp.max(s, axis=-1, keepdims=True))
        probs.append(e / jnp.sum(e, axis=-1, keepdims=True))
    w = probs[0] - lam * probs[1]
    o = _dot(w.astype(BF16), v)
    return _rms(o, DA_SUBLN_EPS) * sg * (1.0 - lam_init)


def _da_lat_kernel(lam_init, tc, tl, q_ref, kc_ref, vc_ref, kl_ref, vl_ref, cos_ref, sin_ref, lam_ref, sg_ref,
                   o_ref, k_s, v_s):
    qi = pl.program_id(2)
    tq = q_ref.shape[0]

    @pl.when(qi == 0)
    def _():
        k_s[0:tc, :] = kc_ref[...].astype(BF16)
        v_s[0:tc, :] = vc_ref[...].astype(BF16)
        k_s[tc:tc + tl, :] = _rope(kl_ref[...], cos_ref[...], sin_ref[...]).astype(BF16)
        v_s[tc:tc + tl, :] = vl_ref[...].astype(BF16)

    r0 = pl.multiple_of(qi * tq, tq)
    q = _rope(q_ref[...], cos_ref[pl.ds(r0, tq), :], sin_ref[pl.ds(r0, tq), :]) * DA_SCALE
    o = _diff_attend(q, k_s[...], v_s[...], _lambda(lam_ref, lam_init), sg_ref[...], lam_init)
    o_ref[...] = o.astype(o_ref.dtype)


def _da_ctx_kernel(lam_init, q_ref, k_ref, v_ref, lam_ref, sg_ref, o_ref):
    o = _diff_attend(q_ref[...] * DA_SCALE, k_ref[...].astype(BF16), v_ref[...].astype(BF16),
                     _lambda(lam_ref, lam_init), sg_ref[...], lam_init)
    o_ref[...] = o.astype(o_ref.dtype)


def _diff_attn(p_c, p_l, cos, sin, lam_p, sg, layer, need_ctx, tq):
    bsz, tc, _ = p_c.shape
    tl = p_l.shape[1]
    nh = DA_HEADS
    w = 2 * DA_HD
    lam_init = 0.8 - 0.6 * math.exp(-0.3 * layer)
    small = [pl.BlockSpec(lam_p.shape, lambda *a: (0, 0)), pl.BlockSpec((1, w), lambda *a: (0, 0))]
    o_l = pl.pallas_call(
        functools.partial(_da_lat_kernel, lam_init, tc, tl),
        out_shape=jax.ShapeDtypeStruct((bsz, tl, nh * w), BF16),
        grid=(bsz, nh, tl // tq),
        in_specs=[pl.BlockSpec((None, tq, w), lambda b, h, i: (b, i, h)),
                  pl.BlockSpec((None, tc, w), lambda b, h, i: (b, 0, nh + h)),
                  pl.BlockSpec((None, tc, w), lambda b, h, i: (b, 0, 2 * nh + h)),
                  pl.BlockSpec((None, tl, w), lambda b, h, i: (b, 0, nh + h)),
                  pl.BlockSpec((None, tl, w), lambda b, h, i: (b, 0, 2 * nh + h)),
                  pl.BlockSpec((tl, w), lambda b, h, i: (0, 0)),
                  pl.BlockSpec((tl, w), lambda b, h, i: (0, 0))] + small,
        out_specs=pl.BlockSpec((None, tq, w), lambda b, h, i: (b, i, h)),
        scratch_shapes=[pltpu.VMEM((tc + tl, w), BF16), pltpu.VMEM((tc + tl, w), BF16)],
        compiler_params=_cparams("parallel", "parallel", "arbitrary"),
        name="diff_attn_lat",
    )(p_l, p_c, p_c, p_l, p_l, cos, sin, lam_p, sg)
    o_c = None
    if need_ctx:
        o_c = pl.pallas_call(
            functools.partial(_da_ctx_kernel, lam_init),
            out_shape=jax.ShapeDtypeStruct((bsz, tc, nh * w), BF16),
            grid=(bsz, nh),
            in_specs=[pl.BlockSpec((None, tc, w), lambda b, h: (b, 0, h)),
                      pl.BlockSpec((None, tc, w), lambda b, h: (b, 0, nh + h)),
                      pl.BlockSpec((None, tc, w), lambda b, h: (b, 0, 2 * nh + h))] + small,
            out_specs=pl.BlockSpec((None, tc, w), lambda b, h: (b, 0, h)),
            compiler_params=_cparams("parallel", "parallel"),
            name="diff_attn_ctx",
        )(p_c, p_c, p_c, lam_p, sg)
    return o_l, o_c


def _rope_tables(tl):
    rows = tl // GRID_W
    t = np.arange(tl)
    pos = np.stack([t // GRID_W, t % GRID_W], axis=1).astype(np.float32)
    inv_freq = (1.0 / (ROPE_BASE ** (jnp.arange(0, ROPE_AXIS_DIM, 2, dtype=F32) / ROPE_AXIS_DIM)))
    lane = np.arange(LANES)
    d = lane % DA_HD
    axis = d // ROPE_AXIS_DIM
    freq = d % (ROPE_AXIS_DIM // 2)
    sign = np.where((d % ROPE_AXIS_DIM) < ROPE_AXIS_DIM // 2, -1.0, 1.0).astype(np.float32)
    ang = jnp.asarray(pos)[:, axis] * inv_freq[freq][None, :]
    del rows
    return jnp.cos(ang), jnp.sin(ang) * sign[None, :]


def _merge_kernel(x_ref, mod_ref, fh_ref, fr_ref, fd_ref, gt_ref, ph_ref, pr_ref, pd_ref, wo_ref, o_ref):
    d = x_ref.shape[-1]
    g = gt_ref[...]
    m = (_sigmoid(g[:, 0:d]) * _dot(fh_ref[...], ph_ref[...])
         + _sigmoid(g[:, d:2 * d]) * _dot(fr_ref[...], pr_ref[...])
         + _sigmoid(g[:, 2 * d:3 * d]) * _dot(fd_ref[...], pd_ref[...]))
    o_ref[...] = x_ref[...] + mod_ref[2:3, :] * _dot(m.astype(BF16), wo_ref[...])


def _merge(x, mod, fh, fr, fd, gate, ph, pr, pd, wo, tm):
    bsz, t, d = x.shape
    c = fh.shape[-1]
    row = lambda n: pl.BlockSpec((None, tm, n), lambda b, i: (b, i, 0))
    wspec = lambda a: pl.BlockSpec(a.shape, lambda b, i: (0, 0))
    return pl.pallas_call(
        _merge_kernel,
        out_shape=jax.ShapeDtypeStruct(x.shape, F32),
        grid=(bsz, t // tm),
        in_specs=[row(d), pl.BlockSpec((None, 6, d), lambda b, i: (b, 0, 0)), row(c), row(c), row(c), row(3 * d),
                  wspec(ph), wspec(pr), wspec(pd), wspec(wo)],
        out_specs=row(d),
        input_output_aliases={0: 0},
        compiler_params=_cparams("parallel", "parallel"),
        name="merge",
    )(x, mod, fh, fr, fd, gate, ph, pr, pd, wo)


def _ffn_prologue(x_ref, g_ref, mod_ref):
    h = _rms(x_ref[...], NORM_EPS) * g_ref[...]
    return h * (1.0 + mod_ref[4:5, :]) + mod_ref[3:4, :]


def _ffn_epilogue(x_ref, mod_ref, acc, fg_ref, o_ref):
    y = x_ref[...] + mod_ref[5:6, :] * acc
    if fg_ref is not None:
        y = _rms(y, NORM_EPS) * fg_ref[...]
    o_ref[...] = y


def _swiglu_step(h, w1_ref, w3_ref, w2_ref):
    a = _dot(h, w1_ref[...])
    u = _silu(a) * _dot(h, w3_ref[...])
    return _dot(u.astype(BF16), w2_ref[...])


def _ffn_kernel(final, x_ref, g_ref, mod_ref, w1_ref, w3_ref, w2_ref, *rest):
    fg_ref, (o_ref, h_s, acc_s) = (rest[0], rest[1:]) if final else (None, rest)
    f = pl.program_id(2)

    @pl.when(f == 0)
    def _():
        h_s[...] = _ffn_prologue(x_ref, g_ref, mod_ref).astype(BF16)
        acc_s[...] = jnp.zeros(acc_s.shape, F32)

    acc_s[...] += _swiglu_step(h_s[...], w1_ref, w3_ref, w2_ref)

    @pl.when(f == pl.num_programs(2) - 1)
    def _():
        _ffn_epilogue(x_ref, mod_ref, acc_s[...], fg_ref, o_ref)


def _ffn(x, g, mod, w1, w3, w2, final_g, tm, tf):
    bsz, t, d = x.shape
    fdim = w1.shape[1]
    final = final_g is not None
    in_specs = [pl.BlockSpec((None, tm, d), lambda b, i, f: (b, i, 0)),
                pl.BlockSpec((1, d), lambda b, i, f: (0, 0)),
                pl.BlockSpec((None, 6, d), lambda b, i, f: (b, 0, 0)),
                pl.BlockSpec((d, tf), lambda b, i, f: (0, f)),
                pl.BlockSpec((d, tf), lambda b, i, f: (0, f)),
                pl.BlockSpec((tf, d), lambda b, i, f: (f, 0))]
    args = [x, g, mod, w1, w3, w2]
    if final:
        in_specs.append(pl.BlockSpec((1, d), lambda b, i, f: (0, 0)))
        args.append(final_g)
    return pl.pallas_call(
        functools.partial(_ffn_kernel, final),
        out_shape=jax.ShapeDtypeStruct(x.shape, F32),
        grid=(bsz, t // tm, fdim // tf),
        in_specs=in_specs,
        out_specs=pl.BlockSpec((None, tm, d), lambda b, i, f: (b, i, 0)),
        scratch_shapes=[pltpu.VMEM((tm, d), BF16), pltpu.VMEM((tm, d), F32)],
        compiler_params=_cparams("parallel", "parallel", "arbitrary"),
        name="ffn",
    )(*args)


def _moe_route_kernel(x_ref, g_ref, mod_ref, rt_ref, tri_ref, h_ref, cmb_ref, pos_ref):
    h = _ffn_prologue(x_ref, g_ref, mod_ref)
    h_ref[...] = h.astype(BF16)
    lane = lax.broadcasted_iota(jnp.int32, cmb_ref.shape, 1).astype(F32)
    logits = jnp.where(lane < N_EXPERTS, _dot_hi(h, rt_ref[...]), -jnp.inf)
    m1 = jnp.max(logits, axis=-1, keepdims=True)
    i1 = jnp.min(jnp.where(logits == m1, lane, float(LANES)), axis=-1, keepdims=True)
    rest_l = jnp.where(lane == i1, -jnp.inf, logits)
    m2 = jnp.max(rest_l, axis=-1, keepdims=True)
    i2 = jnp.min(jnp.where(rest_l == m2, lane, float(LANES)), axis=-1, keepdims=True)
    e2 = jnp.exp(m2 - m1)
    cmb = jnp.where(lane == i1, 1.0 / (1.0 + e2), 0.0) + jnp.where(lane == i2, e2 / (1.0 + e2), 0.0)
    cmb_ref[...] = cmb
    pos_ref[...] = _dot(tri_ref[...], jnp.where(cmb > 0.0, 1.0, 0.0).astype(BF16))


def _moe_expert_kernel(cap, h_ref, cmb_ref, pos_ref, w1_ref, w3_ref, w2_ref, o_ref, acc_s, sel_s, he_s, y_s):
    e = pl.program_id(2)
    f = pl.program_id(3)
    last_f = pl.num_programs(3) - 1
    tm = cmb_ref.shape[0]
    lane = lax.broadcasted_iota(jnp.int32, cmb_ref.shape, 1)
    ce = jnp.sum(jnp.where(lane == e, cmb_ref[...], 0.0), axis=-1, keepdims=True)
    pe = jnp.sum(jnp.where(lane == e, pos_ref[...], 0.0), axis=-1, keepdims=True)
    slot = lax.broadcasted_iota(jnp.int32, (tm, cap), 1).astype(F32)

    def onehot(first):
        return jnp.where((pe - first == slot) & (ce > 0.0), 1.0, 0.0).astype(BF16)

    @pl.when((e == 0) & (f == 0))
    def _():
        acc_s[...] = jnp.zeros(acc_s.shape, F32)

    @pl.when(f == 0)
    def _():
        sel = onehot(0.0)
        sel_s[...] = sel
        he_s[...] = _dot_tn(sel, h_ref[...]).astype(BF16)
        y_s[...] = jnp.zeros(y_s.shape, F32)

    y_s[...] += _swiglu_step(he_s[...], w1_ref, w3_ref, w2_ref)

    @pl.when(f == last_f)
    def _():
        acc_s[...] += ce * _dot(sel_s[...], y_s[...].astype(BF16))

    count = jnp.sum(jnp.where(ce > 0.0, 1.0, 0.0))

    def overflow(j, carry):
        sel = onehot((j * cap).astype(F32))
        hej = _dot_tn(sel, h_ref[...]).astype(BF16)
        yj = _swiglu_step(hej, w1_ref, w3_ref, w2_ref)
        acc_s[...] += ce * _dot(sel, yj.astype(BF16))
        return carry

    lax.fori_loop(1, jnp.ceil(count / cap).astype(jnp.int32), overflow, 0)

    @pl.when((e == pl.num_programs(2) - 1) & (f == last_f))
    def _():
        o_ref[...] = acc_s[...]


def _moe_finish_kernel(final, x_ref, mod_ref, y_ref, *rest):
    fg_ref, o_ref = rest if final else (None, rest[0])
    _ffn_epilogue(x_ref, mod_ref, y_ref[...], fg_ref, o_ref)


def _moe(x, g, mod, router, w1, w3, w2, final_g, tm, tf, cap):
    bsz, t, d = x.shape
    ne, _, fdim = w1.shape
    final = final_g is not None
    tri = jnp.asarray(np.tril(np.ones((tm, tm), np.float32), -1), BF16)
    row = lambda n: pl.BlockSpec((None, tm, n), lambda b, i: (b, i, 0))
    h, cmb, pos = pl.pallas_call(
        _moe_route_kernel,
        out_shape=[jax.ShapeDtypeStruct((bsz, t, d), BF16), jax.ShapeDtypeStruct((bsz, t, LANES), F32),
                   jax.ShapeDtypeStruct((bsz, t, LANES), F32)],
        grid=(bsz, t // tm),
        in_specs=[row(d), pl.BlockSpec((1, d), lambda b, i: (0, 0)), pl.BlockSpec((None, 6, d), lambda b, i: (b, 0, 0)),
                  pl.BlockSpec((d, LANES), lambda b, i: (0, 0)), pl.BlockSpec((tm, tm), lambda b, i: (0, 0))],
        out_specs=[row(d), row(LANES), row(LANES)],
        compiler_params=_cparams("parallel", "parallel"),
        name="moe_route",
    )(x, g, mod, router, tri)
    row4 = lambda n: pl.BlockSpec((None, tm, n), lambda b, i, e, f: (b, i, 0))
    y = pl.pallas_call(
        functools.partial(_moe_expert_kernel, cap),
        out_shape=jax.ShapeDtypeStruct((bsz, t, d), F32),
        grid=(bsz, t // tm, ne, fdim // tf),
        in_specs=[row4(d), row4(LANES), row4(LANES),
                  pl.BlockSpec((None, d, tf), lambda b, i, e, f: (e, 0, f)),
                  pl.BlockSpec((None, d, tf), lambda b, i, e, f: (e, 0, f)),
                  pl.BlockSpec((None, tf, d), lambda b, i, e, f: (e, f, 0))],
        out_specs=row4(d),
        scratch_shapes=[pltpu.VMEM((tm, d), F32), pltpu.VMEM((tm, cap), BF16), pltpu.VMEM((cap, d), BF16),
                        pltpu.VMEM((cap, d), F32)],
        compiler_params=_cparams("parallel", "parallel", "arbitrary", "arbitrary"),
        name="moe_experts",
    )(h, cmb, pos, w1, w3, w2)
    tmf = _row_tile(t, 512)
    rowf = lambda n: pl.BlockSpec((None, tmf, n), lambda b, i: (b, i, 0))
    in_specs = [rowf(d), pl.BlockSpec((None, 6, d), lambda b, i: (b, 0, 0)), rowf(d)]
    args = [x, mod, y]
    if final:
        in_specs.append(pl.BlockSpec((1, d), lambda b, i: (0, 0)))
        args.append(final_g)
    return pl.pallas_call(
        functools.partial(_moe_finish_kernel, final),
        out_shape=jax.ShapeDtypeStruct(x.shape, F32),
        grid=(bsz, t // tmf),
        in_specs=in_specs,
        out_specs=rowf(d),
        compiler_params=_cparams("parallel", "parallel"),
        name="moe_finish",
    )(*args)


def _moe_capacity(tm):
    even = tm * TOP_K // N_EXPERTS
    return -(-(even + even // 4) // 16) * 16


def _row_tile(t, pref):
    tm = min(pref, t)
    while t % tm:
        tm //= 2
    return tm


def kernel(x, c, ctx, c_ctx, ada_w, ada_b, norm_mix_g, norm_ffn_g, final_norm_g, w_in, hg_lb_logits, hg_norm_g,
           hg_proj, rw_mu, rw_w0, rw_w2, rw_a0, rw_a2, rw_g2, rw_k_k, rw_k_a, rw_r_k, rw_ln_g, rw_ln_b, rw_proj,
           da_lambda, da_subln_g, da_proj, w_out, ffn_w1, ffn_w3, ffn_w2, moe_router, moe_w1, moe_w3, moe_w2):
    bsz, tl, d = x.shape
    tc = ctx.shape[1]
    depth = ada_w.shape[0]
    hg_cols = 5 * HG_HEADS * HG_DK
    rw_cols = 3 * RW_C + 2 * RW_W_RANK + RW_A_RANK + RW_G_RANK
    da_cols = 3 * DA_HEADS * 2 * DA_HD
    o_rw = hg_cols
    o_da = o_rw + rw_cols
    o_gt = o_da + da_cols

    ad_lo = 3 * RW_C + 2 * RW_W_RANK
    ad_hi = ad_lo + RW_A_RANK

    def rw_reorder(a):
        pad = jnp.zeros(a.shape[:-1] + ((-rw_cols) % LANES,), a.dtype)
        return jnp.concatenate([a[..., :ad_lo], a[..., ad_hi:], a[..., ad_lo:ad_hi], pad], axis=-1)

    cvec = jnp.zeros((16, d), F32).at[:bsz].set(c).at[bsz].set(c_ctx)
    cos, sin = _rope_tables(tl)
    head_id = np.arange(RW_C) // RW_HD
    hsum = jnp.asarray((head_id[:, None] == head_id[None, :]).astype(np.float32), BF16)
    hmean = (hsum.astype(F32) / RW_HD).astype(BF16)
    zpad = lambda a, rows: jnp.concatenate([a, jnp.zeros((rows - a.shape[0],) + a.shape[1:], a.dtype)], axis=0)

    tm_l = _row_tile(tl, 512)
    tm_c = _row_tile(tc, 256)
    xl, xc = x, ctx
    for l in range(depth):
        need_ctx = l < depth - 1
        last = l == depth - 1
        mods = _adaln(cvec, ada_w[l], ada_b[l])
        mod_l = mods[:bsz].reshape(bsz, 6, d)
        mod_c = jnp.broadcast_to(mods[bsz].reshape(1, 6, d), (bsz, 6, d))

        wl = w_in[l]
        w_hg = wl[:, :o_rw].astype(BF16)
        w_rw = rw_reorder(wl[:, o_rw:o_da]).astype(BF16)
        w_da = wl[:, o_da:o_gt].astype(BF16)
        w_gt = wl[:, o_gt:].astype(BF16)
        gmix = norm_mix_g[l].reshape(1, d)
        proj = lambda xx, mod, w, tm: _inproj(xx, gmix, mod, w, tm)

        fh_c, fh_l = _hgrn2(proj(xc, mod_c, w_hg, tm_c), proj(xl, mod_l, w_hg, tm_l),
                            hg_lb_logits, hg_norm_g[l].reshape(1, HG_DK), l)

        mu = rw_reorder(rw_mu[l]).reshape(1, -1)
        consts = (mu, rw_w0[l], zpad(rw_w2[l, 0], LANES),
                  jnp.concatenate([jnp.zeros_like(rw_w2[l, 1]), rw_w2[l, 1]], axis=0),
                  rw_a0[l].reshape(1, RW_C), zpad(rw_a2[l], LANES), rw_g2[l],
                  rw_k_k[l].reshape(1, RW_C), rw_k_a[l].reshape(1, RW_C), hsum)
        prep_c = _rwkv_prep(proj(xc, mod_c, w_rw, tm_c), consts, tm_c)
        prep_l = _rwkv_prep(proj(xl, mod_l, w_rw, tm_l), consts, _row_tile(tl, 256))
        s0 = jnp.zeros((bsz, 2, RW_HD, RW_C), F32)
        yc_f, yc_b, s_ctx = _rwkv_scan(prep_c, s0, _row_tile(tc, 256))
        yl_f, yl_b, _ = _rwkv_scan(prep_l, s_ctx, _row_tile(tl, 256))
        fin = lambda yf, yb, prep, tm: _rwkv_fin(yf, yb, prep, rw_r_k[l].reshape(1, RW_C),
                                                 rw_ln_g[l].reshape(1, RW_C), rw_ln_b[l].reshape(1, RW_C),
                                                 hmean, hsum, tm)
        fr_l = fin(yl_f, yl_b, prep_l, _row_tile(tl, 256))

        pda_c = proj(xc, mod_c, w_da, tm_c)
        pda_l = proj(xl, mod_l, w_da, tm_l)
        fd_l, fd_c = _diff_attn(pda_c, pda_l, cos, sin, da_lambda[l], da_subln_g[l].reshape(1, -1), l, need_ctx,
                                _row_tile(tl, 256))

        projs = (hg_proj[l].astype(BF16), rw_proj[l].astype(BF16), da_proj[l].astype(BF16), w_out[l].astype(BF16))
        xl = _merge(xl, mod_l, fh_l, fr_l, fd_l, proj(xl, mod_l, w_gt, tm_l), *projs, tm_l)
        if need_ctx:
            fr_c = fin(yc_f, yc_b, prep_c, tm_c)
            xc = _merge(xc, mod_c, fh_c, fr_c, fd_c, proj(xc, mod_c, w_gt, tm_c), *projs, tm_c)

        gffn = norm_ffn_g[l].reshape(1, d)
        fg = final_norm_g.reshape(1, d) if last else None
        j = l // 2
        if l % 2 == 0:
            w1, w3, w2 = ffn_w1[j].astype(BF16), ffn_w3[j].astype(BF16), ffn_w2[j].astype(BF16)
            tf = w1.shape[1] // 2
            xl = _ffn(xl, gffn, mod_l, w1, w3, w2, fg, tm_l, tf)
            if need_ctx:
                xc = _ffn(xc, gffn, mod_c, w1, w3, w2, None, tm_c, tf)
        else:
            w1, w3, w2 = moe_w1[j].astype(BF16), moe_w3[j].astype(BF16), moe_w2[j].astype(BF16)
            tf = w1.shape[2] // 2
            router = jnp.pad(moe_router[j], ((0, 0), (0, LANES - N_EXPERTS)))
            tm_e = _row_tile(tl, 1024)
            xl = _moe(xl, gffn, mod_l, router, w1, w3, w2, fg, tm_e, tf, _moe_capacity(tm_e))
            if need_ctx:
                xc = _moe(xc, gffn, mod_c, router, w1, w3, w2, None, tm_c, tf, _moe_capacity(tm_c))
    if depth == 0:
        raise ValueError("depth must be positive")
    return xl
```

```python
import functools
import math

import numpy as np
import jax
import jax.numpy as jnp
from jax import lax
from jax.experimental import pallas as pl
from jax.experimental.pallas import tpu as pltpu

F32 = jnp.float32
BF16 = jnp.bfloat16

GRID_W = 64
HG_HEADS = 4
HG_DK = 128
RW_HEADS = 8
RW_HD = 64
RW_C = RW_HEADS * RW_HD
RW_W_RANK = 64
RW_A_RANK = 64
RW_G_RANK = 128
RW_LN_EPS = 64e-5
DA_HEADS = 4
DA_HD = 64
DA_SUBLN_EPS = 1e-5
DA_SCALE = DA_HD ** -0.5
ROPE_AXIS_DIM = DA_HD // 2
ROPE_BASE = 10000.0
N_EXPERTS = 8
TOP_K = 2
NORM_EPS = 1e-6

LANES = 128
PREP_TILE = 256
RW_CHUNK = 16
RW_BATCH = 1
VMEM_LIMIT = 52 * 1024 * 1024


def _cparams(*sem):
    return pltpu.CompilerParams(dimension_semantics=sem, vmem_limit_bytes=VMEM_LIMIT)


def _dot(a, b):
    return jnp.dot(a, b, preferred_element_type=F32)


def _dot_nt(a, b):
    return lax.dot_general(a, b, (((1,), (1,)), ((), ())), preferred_element_type=F32)


def _dot_tn(a, b):
    return lax.dot_general(a, b, (((0,), (0,)), ((), ())), preferred_element_type=F32)


def _split2(x):
    hi = x.astype(BF16)
    lo = (x - hi.astype(F32)).astype(BF16)
    return hi, lo


def _split3(x):
    hi = x.astype(BF16)
    r = x - hi.astype(F32)
    mid = r.astype(BF16)
    lo = (r - mid.astype(F32)).astype(BF16)
    return hi, mid, lo


def _dot_hi(a, b):
    ah, al = _split2(a)
    bh, bl = _split2(b)
    return _dot(ah, bh) + (_dot(ah, bl) + _dot(al, bh))


def _dot_x3(x, m):
    hi, mid, lo = _split3(x)
    return _dot(hi, m) + (_dot(mid, m) + _dot(lo, m))


def _dot_m3(m, x):
    hi, mid, lo = _split3(x)
    return _dot(m, hi) + (_dot(m, mid) + _dot(m, lo))


def _sigmoid(x):
    return jax.nn.sigmoid(x)


def _silu(x):
    return x * jax.nn.sigmoid(x)


def _log_sigmoid(z):
    return jnp.minimum(z, 0.0) - jnp.log1p(jnp.exp(-jnp.abs(z)))


def _softplus(x):
    return jnp.maximum(x, 0.0) + jnp.log1p(jnp.exp(-jnp.abs(x)))


def _rms(x, eps):
    return x * lax.rsqrt(jnp.mean(x * x, axis=-1, keepdims=True) + eps)


def _adaln_kernel(c_ref, w_ref, b_ref, o_ref):
    o_ref[...] = _dot_hi(_silu(c_ref[...]), w_ref[...]) + b_ref[...]


def _adaln(cvec, w, b):
    rows, d = cvec.shape
    n = w.shape[1]
    tn = n // 4
    return pl.pallas_call(
        _adaln_kernel,
        out_shape=jax.ShapeDtypeStruct((rows, n), F32),
        grid=(n // tn,),
        in_specs=[pl.BlockSpec((rows, d), lambda j: (0, 0)),
                  pl.BlockSpec((d, tn), lambda j: (0, j)),
                  pl.BlockSpec((1, tn), lambda j: (0, j))],
        out_specs=pl.BlockSpec((rows, tn), lambda j: (0, j)),
        compiler_params=_cparams("arbitrary"),
        name="adaln",
    )(cvec, w, b.reshape(1, n))


def _inproj_kernel(x_ref, g_ref, mod_ref, w_ref, o_ref):
    h = _rms(x_ref[...], NORM_EPS) * g_ref[...]
    h = h * (1.0 + mod_ref[1:2, :]) + mod_ref[0:1, :]
    o_ref[...] = _dot(h.astype(BF16), w_ref[...])


def _inproj(x, g, mod, w, tm):
    bsz, t, d = x.shape
    n = w.shape[1]
    return pl.pallas_call(
        _inproj_kernel,
        out_shape=jax.ShapeDtypeStruct((bsz, t, n), F32),
        grid=(bsz, t // tm),
        in_specs=[pl.BlockSpec((None, tm, d), lambda b, i: (b, i, 0)),
                  pl.BlockSpec((1, d), lambda b, i: (0, 0)),
                  pl.BlockSpec((None, 6, d), lambda b, i: (b, 0, 0)),
                  pl.BlockSpec((d, n), lambda b, i: (0, 0))],
        out_specs=pl.BlockSpec((None, tm, n), lambda b, i: (b, i, 0)),
        compiler_params=_cparams("parallel", "parallel"),
        name="inproj",
    )(x, g, mod, w)


def _hgrn2_kernel(layer, tc, tl,
                  qc_ref, ffc_ref, fbc_ref, ic_ref, gc_ref,
                  ql_ref, ffl_ref, fbl_ref, il_ref, gl_ref,
                  lbl_ref, ng_ref, msk_ref,
                  oc_ref, ol_ref,
                  v_s, qe_s, ke_s, et_s, o_s, st_s):
    nct = tc // PREP_TILE
    nlt = tl // PREP_TILE
    pt = PREP_TILE
    levels = [2 ** j for j in range(1, pt.bit_length())]

    lg = lbl_ref[...]
    mx = jnp.max(lg, axis=0, keepdims=True)
    ex = jnp.exp(lg - mx)
    pr = ex / jnp.sum(ex, axis=0, keepdims=True)
    lb = jnp.zeros(lg.shape[1:], F32)
    for j in range(1, layer + 1):
        lb = lb + pr[j]

    ri = lax.broadcasted_iota(jnp.int32, (pt, pt), 0)
    ci = lax.broadcasted_iota(jnp.int32, (pt, pt), 1)
    tri = (jnp.where(ci <= ri, 1.0, 0.0).astype(BF16), jnp.where(ci >= ri, 1.0, 0.0).astype(BF16))
    sub = lax.broadcasted_iota(jnp.int32, (pt // 8, 8, LANES), 1)

    def midpoint(c, g, d):
        r = g // 2 - 1 + d
        if g >= 8:
            c3 = c.reshape(pt // g, g, LANES)
            return jnp.broadcast_to(c3[:, r:r + 1, :], c3.shape).reshape(pt, LANES)
        c3 = c.reshape(pt // 8, 8, LANES)
        m = jnp.broadcast_to(c3[:, r:r + 1, :], c3.shape)
        for j in range(1, 8 // g):
            m = jnp.where(sub >= j * g, jnp.broadcast_to(c3[:, j * g + r:j * g + r + 1, :], c3.shape), m)
        return m.reshape(pt, LANES)

    def prep(q_ref, ff_ref, fb_ref, i_ref, src, dst, tile):
        q = _silu(q_ref[pl.ds(src, pt), :])
        v = i_ref[pl.ds(src, pt), :]
        vb = v.astype(BF16)
        v_s[pl.ds(dst, pt), :] = vb
        for d, z_ref in enumerate((ff_ref, fb_ref)):
            z = z_ref[pl.ds(src, pt), :]
            lbd = lb[d:d + 1, :]
            a = jnp.log(lbd)
            b = jnp.log1p(-lbd) + _log_sigmoid(z)
            lf = jnp.maximum(a, b) + jnp.log1p(jnp.exp(-jnp.abs(a - b)))
            k = (1.0 - lbd) * _sigmoid(-z)
            c = _dot_m3(tri[d], lf)
            tot = c[pt - 1:pt, :] if d == 0 else c[0:1, :]
            att = jnp.zeros((pt, pt), F32)
            for lvl, g in enumerate(levels):
                e = jnp.exp(-jnp.abs(c - midpoint(c, g, d)))
                att = att + _dot_nt((q * e).astype(BF16), (k * e).astype(BF16)) * msk_ref[d, lvl]
            o = _dot(att.astype(BF16), vb) + jnp.sum(q * k, axis=1, keepdims=True) * v
            o_s[d, pl.ds(dst, pt), :] = o
            qe_s[d, pl.ds(dst, pt), :] = (q * jnp.exp(c)).astype(BF16)
            ke_s[d, pl.ds(dst, pt), :] = (k * jnp.exp(tot - c)).astype(BF16)
            et_s[d, pl.ds(pl.multiple_of(tile * 8, 8), 8), :] = jnp.broadcast_to(jnp.exp(tot), (8, LANES))

    for j in range(nct):
        prep(qc_ref, ffc_ref, fbc_ref, ic_ref, j * pt, j * pt, j)

    def prep_lat(j, carry):
        src = pl.multiple_of(j * pt, pt)
        prep(ql_ref, ffl_ref, fbl_ref, il_ref, src, pl.multiple_of(tc + j * pt, pt), nct + j)
        return carry

    lax.fori_loop(0, nlt, prep_lat, 0)

    st_s[...] = jnp.zeros(st_s.shape, F32)

    def carry_state(d, tile):
        sl = pl.ds(pl.multiple_of(tile * pt, pt), pt)
        st = st_s[d]
        o_s[d, sl, :] = o_s[d, sl, :] + _dot_nt(qe_s[d, sl, :], st.astype(BF16))
        upd = _dot_tn(v_s[sl, :], ke_s[d, sl, :])
        st_s[d] = st * et_s[d, pl.ds(pl.multiple_of(tile * 8, 8), 1), :] + upd

    def step(i, carry):
        carry_state(0, i)
        carry_state(1, jnp.where(i < nct, nct - 1 - i, nct + nlt - 1 - (i - nct)))
        return carry

    lax.fori_loop(0, nct + nlt, step, 0)

    ng = ng_ref[...]

    def fin(g_ref, o_ref, src, dst):
        o = o_s[0, pl.ds(dst, pt), :] + o_s[1, pl.ds(dst, pt), :]
        y = _rms(o, NORM_EPS) * ng
        o_ref[pl.ds(src, pt), :] = (y * _silu(g_ref[pl.ds(src, pt), :])).astype(o_ref.dtype)

    for j in range(tc // pt):
        fin(gc_ref, oc_ref, j * pt, j * pt)

    def fin_lat(j, carry):
        fin(gl_ref, ol_ref, pl.multiple_of(j * pt, pt), pl.multiple_of(tc + j * pt, pt))
        return carry

    lax.fori_loop(0, tl // pt, fin_lat, 0)


def _gla_level_masks():
    pt = PREP_TILE
    t = np.arange(pt)[:, None]
    s = np.arange(pt)[None, :]
    out = np.zeros((2, pt.bit_length() - 1, pt, pt), np.float32)
    for lvl in range(out.shape[1]):
        g = 2 << lvl
        same = (t // g) == (s // g)
        out[0, lvl] = same & (t % g >= g // 2) & (s % g < g // 2)
        out[1, lvl] = same & (t % g < g // 2) & (s % g >= g // 2)
    return jnp.asarray(out)


def _hgrn2(p_c, p_l, lb_logits, norm_g, layer):
    bsz, tc, _ = p_c.shape
    tl = p_l.shape[1]
    t = tc + tl
    depth = lb_logits.shape[0]
    nh = HG_HEADS
    w = HG_DK

    def col(tx, j):
        return pl.BlockSpec((None, tx, w), lambda b, h, j=j: (b, 0, j * nh + h))

    masks = _gla_level_masks()
    in_specs = ([col(tc, j) for j in range(5)] + [col(tl, j) for j in range(5)]
                + [pl.BlockSpec((depth, 2, w), lambda b, h: (0, 0, h)),
                   pl.BlockSpec((1, w), lambda b, h: (0, 0)),
                   pl.BlockSpec(masks.shape, lambda b, h: (0, 0, 0, 0))])
    out_specs = [pl.BlockSpec((None, tc, w), lambda b, h: (b, 0, h)),
                 pl.BlockSpec((None, tl, w), lambda b, h: (b, 0, h))]
    scratch = [pltpu.VMEM((t, w), BF16),
               pltpu.VMEM((2, t, w), BF16), pltpu.VMEM((2, t, w), BF16),
               pltpu.VMEM((2, 8 * (t // PREP_TILE), w), F32), pltpu.VMEM((2, t, w), F32),
               pltpu.VMEM((2, w, w), F32)]
    return pl.pallas_call(
        functools.partial(_hgrn2_kernel, layer, tc, tl),
        out_shape=[jax.ShapeDtypeStruct((bsz, tc, nh * w), BF16),
                   jax.ShapeDtypeStruct((bsz, tl, nh * w), BF16)],
        grid=(bsz, nh),
        in_specs=in_specs,
        out_specs=out_specs,
        scratch_shapes=scratch,
        compiler_params=_cparams("parallel", "parallel"),
        name="hgrn2",
    )(p_c, p_c, p_c, p_c, p_c, p_l, p_l, p_l, p_l, p_l, lb_logits, norm_g, masks)


def _rwkv_prep_kernel(nt, p_ref, pp_ref, pn_ref, mu_ref, w0_ref, w2f_ref, w2b_ref, a0_ref, a2_ref, g2_ref,
                      kk_ref, ka_ref, hsum_ref,
                      r_ref, lwf_ref, lwb_ref, k_ref, an_ref, bb_ref, v_ref, g_ref):
    i = pl.program_id(1)
    p = p_ref[...]
    tm = p.shape[0]
    rows = lax.broadcasted_iota(jnp.int32, p.shape, 0)
    first = jnp.where(i > 0, pp_ref[7:8, :], 0.0)
    last = jnp.where(i < nt - 1, pn_ref[0:1, :], 0.0)
    prev = jnp.where(rows == 0, first, pltpu.roll(p, 1, 0))
    nxt = jnp.where(rows == tm - 1, last, pltpu.roll(p, tm - 1, 0))
    xs = p + mu_ref[...] * (0.5 * (prev + nxt) - p)

    c = RW_C
    r = xs[:, 0:c]
    k = xs[:, c:2 * c]
    v = xs[:, 2 * c:3 * c]
    wd = jnp.tanh(xs[:, 3 * c:3 * c + LANES])
    gd = _sigmoid(xs[:, 3 * c + LANES:3 * c + 2 * LANES])
    ad = xs[:, 3 * c + 2 * LANES:3 * c + 3 * LANES]

    a = _sigmoid(a0_ref[...] + _dot_hi(ad, a2_ref[...]))
    kk = k * kk_ref[...]
    nrm = jnp.sqrt(_dot_x3(kk * kk, hsum_ref[...]))
    kk = kk / jnp.maximum(nrm, 1e-12)
    kp = k * (1.0 + (a - 1.0) * ka_ref[...])
    g = _dot_hi(gd, g2_ref[...])

    def log_decay(w2_ref, d):
        wlog = -_softplus(-(w0_ref[d:d + 1, :] + _dot_hi(wd, w2_ref[...]))) - 0.5
        return -jnp.exp(wlog)

    r_ref[...] = r
    lwf_ref[...] = log_decay(w2f_ref, 0)
    lwb_ref[...] = log_decay(w2b_ref, 1)
    k_ref[...] = kp
    an_ref[...] = -kk
    bb_ref[...] = kk * a
    v_ref[...] = v
    g_ref[...] = g


def _rwkv_prep(p, consts, tm):
    bsz, t, n = p.shape
    nt = t // tm
    c = RW_C
    hb = tm // 8
    full = lambda a: pl.BlockSpec(a.shape, lambda b, i, nd=a.ndim: (0,) * nd)
    in_specs = ([pl.BlockSpec((None, tm, n), lambda b, i: (b, i, 0)),
                 pl.BlockSpec((None, 8, n), lambda b, i: (b, jnp.maximum(i * hb - 1, 0), 0)),
                 pl.BlockSpec((None, 8, n), lambda b, i: (b, jnp.minimum((i + 1) * hb, t // 8 - 1), 0))]
                + [full(a) for a in consts])
    out_specs = [pl.BlockSpec((None, tm, c), lambda b, i: (b, i, 0))] * 8
    out_shape = [jax.ShapeDtypeStruct((bsz, t, c), F32)] * 8
    return pl.pallas_call(
        functools.partial(_rwkv_prep_kernel, nt),
        out_shape=out_shape,
        grid=(bsz, nt),
        in_specs=in_specs,
        out_specs=out_specs,
        compiler_params=_cparams("parallel", "parallel"),
        name="rwkv_prep",
    )(p, p, p, *consts)


def _rwkv_scan_kernel(*refs):
    dir_in = (refs[0:6], refs[6:12])
    s0_ref = refs[12]
    y_refs = refs[13:15]
    sout_ref = refs[15]
    s_scr, x6_s = refs[16:]
    nh, hd, c = RW_HEADS, RW_HD, RW_CHUNK
    n = c * nh
    i = pl.program_id(1)
    nbat = dir_in[0][0].shape[0]
    nch = dir_in[0][0].shape[1] // c

    @pl.when(i == 0)
    def _():
        s_scr[...] = s0_ref[...]

    tb = nch * c
    nl = nh * hd // LANES
    ti = lax.broadcasted_iota(jnp.int32, (tb, tb), 0)
    tj = lax.broadcasted_iota(jnp.int32, (tb, tb), 1)
    same = (ti // c) == (tj // c)
    m_all = jnp.where(same, 1.0, 0.0).astype(BF16)
    tri = (jnp.where(same & (tj <= ti), 1.0, 0.0).astype(BF16), jnp.where(same & (tj >= ti), 1.0, 0.0).astype(BF16))

    for q in range(nbat):
        for d in range(2):
            r_ref, lw_ref, k_ref, a_ref, b_ref, _ = dir_in[d]
            lw = lw_ref[q]
            g = _dot_m3(tri[d], lw)
            tot = _dot_m3(m_all, lw)
            ineg = jnp.exp(-g)
            etg = jnp.exp(tot - g)
            a, b, k = a_ref[q], b_ref[q], k_ref[q]
            groups = (a * jnp.exp(g - lw), r_ref[q] * jnp.exp(g), b * ineg, k * ineg, b * etg, k * etg, jnp.exp(tot))
            for j, val in enumerate(groups):
                for m in range(nl):
                    x6_s[q, d, j, m] = val[:, m * LANES:(m + 1) * LANES]

    tr = lax.broadcasted_iota(jnp.int32, (n, n), 0) // nh
    ts = lax.broadcasted_iota(jnp.int32, (n, n), 1) // nh
    strict = (ts < tr, ts > tr)
    incl = (ts <= tr, ts >= tr)
    rc = lax.broadcasted_iota(jnp.int32, (n, c), 0) // nh
    cc = lax.broadcasted_iota(jnp.int32, (n, c), 1)
    strict_c = (cc < rc, cc > rc)
    hrow = lax.broadcasted_iota(jnp.int32, (nh, nh * hd), 0)
    hcol = lax.broadcasted_iota(jnp.int32, (nh, nh * hd), 1) // hd
    own = jnp.where(hrow == hcol, 1.0, 0.0)

    def chunk(q, d, c0):
        v_ref = dir_in[d][5]
        sls = pl.ds(pl.multiple_of(c0 * nh, n), n)

        def rows(j, start, size):
            return jnp.concatenate([x6_s[q, d, j, m, pl.ds(start, size), :] for m in range(nl)], axis=1)

        xx = jnp.concatenate([rows(j, c0 + t, 1) * own for j in range(6) for t in range(c)], axis=0).astype(BF16)
        lx, rx, bx = xx[0:2 * n], xx[2 * n:4 * n], xx[4 * n:6 * n]
        gm = _dot_nt(lx, rx)
        aab = jnp.where(strict_c[d], _dot_nt(lx[0:n], rows(2, c0, c).astype(BF16)), 0.0)
        aak = jnp.where(strict[d], gm[0:n, n:2 * n], 0.0)
        arb = jnp.where(incl[d], gm[n:2 * n, 0:n], 0.0)
        ark = jnp.where(incl[d], gm[n:2 * n, n:2 * n], 0.0)
        vb = v_ref[q, sls, :].astype(BF16)
        st = s_scr[q, d]
        w = _dot_nt(lx, st.astype(BF16))
        rhs = w[0:n] + _dot(aak.astype(BF16), vb)
        blocks = [rhs[nh * t:nh * (t + 1)] for t in range(c)]
        order = range(c) if d == 0 else range(c - 1, -1, -1)
        for s in order:
            later = range(s + 1, c) if d == 0 else range(0, s)
            for t in later:
                blocks[t] = blocks[t] + aab[nh * t:nh * (t + 1), s:s + 1] * blocks[s]
        u = jnp.concatenate(blocks, axis=0)
        uv = jnp.concatenate([u.astype(BF16), vb], axis=0)
        y_refs[d][q, sls, :] = w[n:2 * n] + _dot(jnp.concatenate([arb, ark], axis=1).astype(BF16), uv)
        s_scr[q, d] = st * rows(6, c0, 1) + _dot_tn(uv, bx)

    def step(j, carry):
        for q in range(nbat):
            chunk(q, 0, pl.multiple_of(j * c, c))
            chunk(q, 1, pl.multiple_of((nch - 1 - j) * c, c))
        return carry

    lax.fori_loop(0, nch, step, 0)

    @pl.when(i == pl.num_programs(1) - 1)
    def _():
        sout_ref[...] = s_scr[...]


def _rwkv_scan(prep, s0, tb):
    r, lwf, lwb, k, an, bb, v = prep[:7]
    bsz, t, ch = r.shape
    nh, hd = RW_HEADS, RW_HD
    nblk = t // tb
    vs = v.reshape(bsz, t * nh, hd)
    nb = RW_BATCH if bsz % RW_BATCH == 0 else 1
    fwd = lambda b, i: (b, i, 0)
    bwd = lambda b, i: (b, nblk - 1 - i, 0)
    nat = lambda im: pl.BlockSpec((nb, tb, ch), im)
    stk = lambda im: pl.BlockSpec((nb, tb * nh, hd), im)
    sspec = pl.BlockSpec((nb, 2, hd, ch), lambda b, i: (b, 0, 0, 0))
    yshape = jax.ShapeDtypeStruct((bsz, t * nh, hd), F32)
    yf, yb, s_out = pl.pallas_call(
        _rwkv_scan_kernel,
        out_shape=[yshape, yshape, jax.ShapeDtypeStruct(s0.shape, F32)],
        grid=(bsz // nb, nblk),
        in_specs=[nat(fwd)] * 5 + [stk(fwd)] + [nat(bwd)] * 5 + [stk(bwd)] + [sspec],
        out_specs=[stk(fwd), stk(bwd), sspec],
        scratch_shapes=[pltpu.VMEM((nb, 2, hd, ch), F32), pltpu.VMEM((nb, 2, 7, ch // LANES, tb, LANES), F32)],
        compiler_params=_cparams("parallel", "arbitrary"),
        name="rwkv_scan",
    )(r, lwf, k, an, bb, vs, r, lwb, k, an, bb, vs, s0)
    return yf.reshape(bsz, t, ch), yb.reshape(bsz, t, ch), s_out


def _rwkv_fin_kernel(yf_ref, yb_ref, r_ref, k_ref, v_ref, g_ref, rk_ref, lng_ref, lnb_ref, hmean_ref, hsum_ref, o_ref):
    y = yf_ref[...] + yb_ref[...]
    r, k, v = r_ref[...], k_ref[...], v_ref[...]
    mean = _dot_x3(y, hmean_ref[...])
    yc = y - mean
    var = _dot_x3(yc * yc, hmean_ref[...])
    yn = yc * lax.rsqrt(var + RW_LN_EPS) * lng_ref[...] + lnb_ref[...]
    bonus = _dot_x3(r * k * rk_ref[...], hsum_ref[...]) * v
    o_ref[...] = ((yn + bonus) * g_ref[...]).astype(o_ref.dtype)


def _rwkv_fin(yf, yb, prep, rk, lng, lnb, hmean, hsum, tm):
    bsz, t, c = yf.shape
    r, _, _, k, _, _, v, g = prep
    tspec = pl.BlockSpec((None, tm, c), lambda b, i: (b, i, 0))
    vec = pl.BlockSpec((1, c), lambda b, i: (0, 0))
    mat = pl.BlockSpec((c, c), lambda b, i: (0, 0))
    return pl.pallas_call(
        _rwkv_fin_kernel,
        out_shape=jax.ShapeDtypeStruct((bsz, t, c), BF16),
        grid=(bsz, t // tm),
        in_specs=[tspec] * 6 + [vec, vec, vec, mat, mat],
        out_specs=tspec,
        compiler_params=_cparams("parallel", "parallel"),
        name="rwkv_fin",
    )(yf, yb, r, k, v, g, rk, lng, lnb, hmean, hsum)


def _rope(x, cos, sin):
    lane = lax.broadcasted_iota(jnp.int32, x.shape, 1)
    partner = jnp.where((lane % ROPE_AXIS_DIM) < ROPE_AXIS_DIM // 2,
                        pltpu.roll(x, LANES - ROPE_AXIS_DIM // 2, 1), pltpu.roll(x, ROPE_AXIS_DIM // 2, 1))
    return x * cos + partner * sin


def _lambda(lam_ref, lam_init):
    lp = lam_ref[...]
    s1 = jnp.sum(lp[0:1, :] * lp[1:2, :], axis=1, keepdims=True)
    s2 = jnp.sum(lp[2:3, :] * lp[3:4, :], axis=1, keepdims=True)
    return jnp.exp(s1) - jnp.exp(s2) + lam_init


def _diff_attend(q, k, v, lam, sg, lam_init):
    lane = lax.broadcasted_iota(jnp.int32, q.shape, 1)
    qb = q.astype(BF16)
    zero = jnp.zeros_like(qb)
    probs = []
    for m in range(2):
        qm = jnp.where((lane // DA_HD) == m, qb, zero)
        s = _dot_nt(qm, k)
        e = jnp.exp(s - jnp.max(s, axis=-1, keepdims=True))
        probs.append(e / jnp.sum(e, axis=-1, keepdims=True))
    w = probs[0] - lam * probs[1]
    o = _dot(w.astype(BF16), v)
    return _rms(o, DA_SUBLN_EPS) * sg * (1.0 - lam_init)


def _da_lat_kernel(lam_init, tc, tl, q_ref, kc_ref, vc_ref, kl_ref, vl_ref, cos_ref, sin_ref, lam_ref, sg_ref,
                   o_ref, k_s, v_s):
    qi = pl.program_id(2)
    tq = q_ref.shape[0]

    @pl.when(qi == 0)
    def _():
        k_s[0:tc, :] = kc_ref[...].astype(BF16)
        v_s[0:tc, :] = vc_ref[...].astype(BF16)
        k_s[tc:tc + tl, :] = _rope(kl_ref[...], cos_ref[...], sin_ref[...]).astype(BF16)
        v_s[tc:tc + tl, :] = vl_ref[...].astype(BF16)

    r0 = pl.multiple_of(qi * tq, tq)
    q = _rope(q_ref[...], cos_ref[pl.ds(r0, tq), :], sin_ref[pl.ds(r0, tq), :]) * DA_SCALE
    o = _diff_attend(q, k_s[...], v_s[...], _lambda(lam_ref, lam_init), sg_ref[...], lam_init)
    o_ref[...] = o.astype(o_ref.dtype)


def _da_ctx_kernel(lam_init, q_ref, k_ref, v_ref, lam_ref, sg_ref, o_ref):
    o = _diff_attend(q_ref[...] * DA_SCALE, k_ref[...].astype(BF16), v_ref[...].astype(BF16),
                     _lambda(lam_ref, lam_init), sg_ref[...], lam_init)
    o_ref[...] = o.astype(o_ref.dtype)


def _diff_attn(p_c, p_l, cos, sin, lam_p, sg, layer, need_ctx, tq):
    bsz, tc, _ = p_c.shape
    tl = p_l.shape[1]
    nh = DA_HEADS
    w = 2 * DA_HD
    lam_init = 0.8 - 0.6 * math.exp(-0.3 * layer)
    small = [pl.BlockSpec(lam_p.shape, lambda *a: (0, 0)), pl.BlockSpec((1, w), lambda *a: (0, 0))]
    o_l = pl.pallas_call(
        functools.partial(_da_lat_kernel, lam_init, tc, tl),
        out_shape=jax.ShapeDtypeStruct((bsz, tl, nh * w), BF16),
        grid=(bsz, nh, tl // tq),
        in_specs=[pl.BlockSpec((None, tq, w), lambda b, h, i: (b, i, h)),
                  pl.BlockSpec((None, tc, w), lambda b, h, i: (b, 0, nh + h)),
                  pl.BlockSpec((None, tc, w), lambda b, h, i: (b, 0, 2 * nh + h)),
                  pl.BlockSpec((None, tl, w), lambda b, h, i: (b, 0, nh + h)),
                  pl.BlockSpec((None, tl, w), lambda b, h, i: (b, 0, 2 * nh + h)),
                  pl.BlockSpec((tl, w), lambda b, h, i: (0, 0)),
                  pl.BlockSpec((tl, w), lambda b, h, i: (0, 0))] + small,
        out_specs=pl.BlockSpec((None, tq, w), lambda b, h, i: (b, i, h)),
        scratch_shapes=[pltpu.VMEM((tc + tl, w), BF16), pltpu.VMEM((tc + tl, w), BF16)],
        compiler_params=_cparams("parallel", "parallel", "arbitrary"),
        name="diff_attn_lat",
    )(p_l, p_c, p_c, p_l, p_l, cos, sin, lam_p, sg)
    o_c = None
    if need_ctx:
        o_c = pl.pallas_call(
            functools.partial(_da_ctx_kernel, lam_init),
            out_shape=jax.ShapeDtypeStruct((bsz, tc, nh * w), BF16),
            grid=(bsz, nh),
            in_specs=[pl.BlockSpec((None, tc, w), lambda b, h: (b, 0, h)),
                      pl.BlockSpec((None, tc, w), lambda b, h: (b, 0, nh + h)),
                      pl.BlockSpec((None, tc, w), lambda b, h: (b, 0, 2 * nh + h))] + small,
            out_specs=pl.BlockSpec((None, tc, w), lambda b, h: (b, 0, h)),
            compiler_params=_cparams("parallel", "parallel"),
            name="diff_attn_ctx",
        )(p_c, p_c, p_c, lam_p, sg)
    return o_l, o_c


def _rope_tables(tl):
    rows = tl // GRID_W
    t = np.arange(tl)
    pos = np.stack([t // GRID_W, t % GRID_W], axis=1).astype(np.float32)
    inv_freq = (1.0 / (ROPE_BASE ** (jnp.arange(0, ROPE_AXIS_DIM, 2, dtype=F32) / ROPE_AXIS_DIM)))
    lane = np.arange(LANES)
    d = lane % DA_HD
    axis = d // ROPE_AXIS_DIM
    freq = d % (ROPE_AXIS_DIM // 2)
    sign = np.where((d % ROPE_AXIS_DIM) < ROPE_AXIS_DIM // 2, -1.0, 1.0).astype(np.float32)
    ang = jnp.asarray(pos)[:, axis] * inv_freq[freq][None, :]
    del rows
    return jnp.cos(ang), jnp.sin(ang) * sign[None, :]


def _merge_kernel(x_ref, mod_ref, fh_ref, fr_ref, fd_ref, gt_ref, ph_ref, pr_ref, pd_ref, wo_ref, o_ref):
    d = x_ref.shape[-1]
    g = gt_ref[...]
    m = (_sigmoid(g[:, 0:d]) * _dot(fh_ref[...], ph_ref[...])
         + _sigmoid(g[:, d:2 * d]) * _dot(fr_ref[...], pr_ref[...])
         + _sigmoid(g[:, 2 * d:3 * d]) * _dot(fd_ref[...], pd_ref[...]))
    o_ref[...] = x_ref[...] + mod_ref[2:3, :] * _dot(m.astype(BF16), wo_ref[...])


def _merge(x, mod, fh, fr, fd, gate, ph, pr, pd, wo, tm):
    bsz, t, d = x.shape
    c = fh.shape[-1]
    row = lambda n: pl.BlockSpec((None, tm, n), lambda b, i: (b, i, 0))
    wspec = lambda a: pl.BlockSpec(a.shape, lambda b, i: (0, 0))
    return pl.pallas_call(
        _merge_kernel,
        out_shape=jax.ShapeDtypeStruct(x.shape, F32),
        grid=(bsz, t // tm),
        in_specs=[row(d), pl.BlockSpec((None, 6, d), lambda b, i: (b, 0, 0)), row(c), row(c), row(c), row(3 * d),
                  wspec(ph), wspec(pr), wspec(pd), wspec(wo)],
        out_specs=row(d),
        input_output_aliases={0: 0},
        compiler_params=_cparams("parallel", "parallel"),
        name="merge",
    )(x, mod, fh, fr, fd, gate, ph, pr, pd, wo)


def _ffn_prologue(x_ref, g_ref, mod_ref):
    h = _rms(x_ref[...], NORM_EPS) * g_ref[...]
    return h * (1.0 + mod_ref[4:5, :]) + mod_ref[3:4, :]


def _ffn_epilogue(x_ref, mod_ref, acc, fg_ref, o_ref):
    y = x_ref[...] + mod_ref[5:6, :] * acc
    if fg_ref is not None:
        y = _rms(y, NORM_EPS) * fg_ref[...]
    o_ref[...] = y


def _swiglu_step(h, w1_ref, w3_ref, w2_ref):
    a = _dot(h, w1_ref[...])
    u = _silu(a) * _dot(h, w3_ref[...])
    return _dot(u.astype(BF16), w2_ref[...])


def _ffn_kernel(final, x_ref, g_ref, mod_ref, w1_ref, w3_ref, w2_ref, *rest):
    fg_ref, (o_ref, h_s, acc_s) = (rest[0], rest[1:]) if final else (None, rest)
    f = pl.program_id(2)

    @pl.when(f == 0)
    def _():
        h_s[...] = _ffn_prologue(x_ref, g_ref, mod_ref).astype(BF16)
        acc_s[...] = jnp.zeros(acc_s.shape, F32)

    acc_s[...] += _swiglu_step(h_s[...], w1_ref, w3_ref, w2_ref)

    @pl.when(f == pl.num_programs(2) - 1)
    def _():
        _ffn_epilogue(x_ref, mod_ref, acc_s[...], fg_ref, o_ref)


def _ffn(x, g, mod, w1, w3, w2, final_g, tm, tf):
    bsz, t, d = x.shape
    fdim = w1.shape[1]
    final = final_g is not None
    in_specs = [pl.BlockSpec((None, tm, d), lambda b, i, f: (b, i, 0)),
                pl.BlockSpec((1, d), lambda b, i, f: (0, 0)),
                pl.BlockSpec((None, 6, d), lambda b, i, f: (b, 0, 0)),
                pl.BlockSpec((d, tf), lambda b, i, f: (0, f)),
                pl.BlockSpec((d, tf), lambda b, i, f: (0, f)),
                pl.BlockSpec((tf, d), lambda b, i, f: (f, 0))]
    args = [x, g, mod, w1, w3, w2]
    if final:
        in_specs.append(pl.BlockSpec((1, d), lambda b, i, f: (0, 0)))
        args.append(final_g)
    return pl.pallas_call(
        functools.partial(_ffn_kernel, final),
        out_shape=jax.ShapeDtypeStruct(x.shape, F32),
        grid=(bsz, t // tm, fdim // tf),
        in_specs=in_specs,
        out_specs=pl.BlockSpec((None, tm, d), lambda b, i, f: (b, i, 0)),
        scratch_shapes=[pltpu.VMEM((tm, d), BF16), pltpu.VMEM((tm, d), F32)],
        compiler_params=_cparams("parallel", "parallel", "arbitrary"),
        name="ffn",
    )(*args)


def _moe_route_kernel(x_ref, g_ref, mod_ref, rt_ref, tri_ref, h_ref, cmb_ref, pos_ref):
    h = _ffn_prologue(x_ref, g_ref, mod_ref)
    h_ref[...] = h.astype(BF16)
    lane = lax.broadcasted_iota(jnp.int32, cmb_ref.shape, 1).astype(F32)
    logits = jnp.where(lane < N_EXPERTS, _dot_hi(h, rt_ref[...]), -jnp.inf)
    m1 = jnp.max(logits, axis=-1, keepdims=True)
    i1 = jnp.min(jnp.where(logits == m1, lane, float(LANES)), axis=-1, keepdims=True)
    rest_l = jnp.where(lane == i1, -jnp.inf, logits)
    m2 = jnp.max(rest_l, axis=-1, keepdims=True)
    i2 = jnp.min(jnp.where(rest_l == m2, lane, float(LANES)), axis=-1, keepdims=True)
    e2 = jnp.exp(m2 - m1)
    cmb = jnp.where(lane == i1, 1.0 / (1.0 + e2), 0.0) + jnp.where(lane == i2, e2 / (1.0 + e2), 0.0)
    cmb_ref[...] = cmb
    pos_ref[...] = _dot(tri_ref[...], jnp.where(cmb > 0.0, 1.0, 0.0).astype(BF16))


def _moe_expert_kernel(cap, h_ref, cmb_ref, pos_ref, w1_ref, w3_ref, w2_ref, o_ref, acc_s, sel_s, he_s, y_s):
    e = pl.program_id(2)
    f = pl.program_id(3)
    last_f = pl.num_programs(3) - 1
    tm = cmb_ref.shape[0]
    lane = lax.broadcasted_iota(jnp.int32, cmb_ref.shape, 1)
    ce = jnp.sum(jnp.where(lane == e, cmb_ref[...], 0.0), axis=-1, keepdims=True)
    pe = jnp.sum(jnp.where(lane == e, pos_ref[...], 0.0), axis=-1, keepdims=True)
    slot = lax.broadcasted_iota(jnp.int32, (tm, cap), 1).astype(F32)

    def onehot(first):
        return jnp.where((pe - first == slot) & (ce > 0.0), 1.0, 0.0).astype(BF16)

    @pl.when((e == 0) & (f == 0))
    def _():
        acc_s[...] = jnp.zeros(acc_s.shape, F32)

    @pl.when(f == 0)
    def _():
        sel = onehot(0.0)
        sel_s[...] = sel
        he_s[...] = _dot_tn(sel, h_ref[...]).astype(BF16)
        y_s[...] = jnp.zeros(y_s.shape, F32)

    y_s[...] += _swiglu_step(he_s[...], w1_ref, w3_ref, w2_ref)

    @pl.when(f == last_f)
    def _():
        acc_s[...] += ce * _dot(sel_s[...], y_s[...].astype(BF16))

    count = jnp.sum(jnp.where(ce > 0.0, 1.0, 0.0))

    def overflow(j, carry):
        sel = onehot((j * cap).astype(F32))
        hej = _dot_tn(sel, h_ref[...]).astype(BF16)
        yj = _swiglu_step(hej, w1_ref, w3_ref, w2_ref)
        acc_s[...] += ce * _dot(sel, yj.astype(BF16))
        return carry

    lax.fori_loop(1, jnp.ceil(count / cap).astype(jnp.int32), overflow, 0)

    @pl.when((e == pl.num_programs(2) - 1) & (f == last_f))
    def _():
        o_ref[...] = acc_s[...]


def _moe_finish_kernel(final, x_ref, mod_ref, y_ref, *rest):
    fg_ref, o_ref = rest if final else (None, rest[0])
    _ffn_epilogue(x_ref, mod_ref, y_ref[...], fg_ref, o_ref)


def _moe(x, g, mod, router, w1, w3, w2, final_g, tm, tf, cap):
    bsz, t, d = x.shape
    ne, _, fdim = w1.shape
    final = final_g is not None
    tri = jnp.asarray(np.tril(np.ones((tm, tm), np.float32), -1), BF16)
    row = lambda n: pl.BlockSpec((None, tm, n), lambda b, i: (b, i, 0))
    h, cmb, pos = pl.pallas_call(
        _moe_route_kernel,
        out_shape=[jax.ShapeDtypeStruct((bsz, t, d), BF16), jax.ShapeDtypeStruct((bsz, t, LANES), F32),
                   jax.ShapeDtypeStruct((bsz, t, LANES), F32)],
        grid=(bsz, t // tm),
        in_specs=[row(d), pl.BlockSpec((1, d), lambda b, i: (0, 0)), pl.BlockSpec((None, 6, d), lambda b, i: (b, 0, 0)),
                  pl.BlockSpec((d, LANES), lambda b, i: (0, 0)), pl.BlockSpec((tm, tm), lambda b, i: (0, 0))],
        out_specs=[row(d), row(LANES), row(LANES)],
        compiler_params=_cparams("parallel", "parallel"),
        name="moe_route",
    )(x, g, mod, router, tri)
    row4 = lambda n: pl.BlockSpec((None, tm, n), lambda b, i, e, f: (b, i, 0))
    y = pl.pallas_call(
        functools.partial(_moe_expert_kernel, cap),
        out_shape=jax.ShapeDtypeStruct((bsz, t, d), F32),
        grid=(bsz, t // tm, ne, fdim // tf),
        in_specs=[row4(d), row4(LANES), row4(LANES),
                  pl.BlockSpec((None, d, tf), lambda b, i, e, f: (e, 0, f)),
                  pl.BlockSpec((None, d, tf), lambda b, i, e, f: (e, 0, f)),
                  pl.BlockSpec((None, tf, d), lambda b, i, e, f: (e, f, 0))],
        out_specs=row4(d),
        scratch_shapes=[pltpu.VMEM((tm, d), F32), pltpu.VMEM((tm, cap), BF16), pltpu.VMEM((cap, d), BF16),
                        pltpu.VMEM((cap, d), F32)],
        compiler_params=_cparams("parallel", "parallel", "arbitrary", "arbitrary"),
        name="moe_experts",
    )(h, cmb, pos, w1, w3, w2)
    tmf = _row_tile(t, 512)
    rowf = lambda n: pl.BlockSpec((None, tmf, n), lambda b, i: (b, i, 0))
    in_specs = [rowf(d), pl.BlockSpec((None, 6, d), lambda b, i: (b, 0, 0)), rowf(d)]
    args = [x, mod, y]
    if final:
        in_specs.append(pl.BlockSpec((1, d), lambda b, i: (0, 0)))
        args.append(final_g)
    return pl.pallas_call(
        functools.partial(_moe_finish_kernel, final),
        out_shape=jax.ShapeDtypeStruct(x.shape, F32),
        grid=(bsz, t // tmf),
        in_specs=in_specs,
        out_specs=rowf(d),
        compiler_params=_cparams("parallel", "parallel"),
        name="moe_finish",
    )(*args)


def _moe_capacity(tm):
    even = tm * TOP_K // N_EXPERTS
    return -(-(even + even // 4) // 16) * 16


def _row_tile(t, pref):
    tm = min(pref, t)
    while t % tm:
        tm //= 2
    return tm


def kernel(x, c, ctx, c_ctx, ada_w, ada_b, norm_mix_g, norm_ffn_g, final_norm_g, w_in, hg_lb_logits, hg_norm_g,
           hg_proj, rw_mu, rw_w0, rw_w2, rw_a0, rw_a2, rw_g2, rw_k_k, rw_k_a, rw_r_k, rw_ln_g, rw_ln_b, rw_proj,
           da_lambda, da_subln_g, da_proj, w_out, ffn_w1, ffn_w3, ffn_w2, moe_router, moe_w1, moe_w3, moe_w2):
    bsz, tl, d = x.shape
    tc = ctx.shape[1]
    depth = ada_w.shape[0]
    hg_cols = 5 * HG_HEADS * HG_DK
    rw_cols = 3 * RW_C + 2 * RW_W_RANK + RW_A_RANK + RW_G_RANK
    da_cols = 3 * DA_HEADS * 2 * DA_HD
    o_rw = hg_cols
    o_da = o_rw + rw_cols
    o_gt = o_da + da_cols

    ad_lo = 3 * RW_C + 2 * RW_W_RANK
    ad_hi = ad_lo + RW_A_RANK

    def rw_reorder(a):
        pad = jnp.zeros(a.shape[:-1] + ((-rw_cols) % LANES,), a.dtype)
        return jnp.concatenate([a[..., :ad_lo], a[..., ad_hi:], a[..., ad_lo:ad_hi], pad], axis=-1)

    cvec = jnp.zeros((16, d), F32).at[:bsz].set(c).at[bsz].set(c_ctx)
    cos, sin = _rope_tables(tl)
    head_id = np.arange(RW_C) // RW_HD
    hsum = jnp.asarray((head_id[:, None] == head_id[None, :]).astype(np.float32), BF16)
    hmean = (hsum.astype(F32) / RW_HD).astype(BF16)
    zpad = lambda a, rows: jnp.concatenate([a, jnp.zeros((rows - a.shape[0],) + a.shape[1:], a.dtype)], axis=0)

    tm_l = _row_tile(tl, 512)
    tm_c = _row_tile(tc, 256)
    xl, xc = x, ctx
    for l in range(depth):
        need_ctx = l < depth - 1
        last = l == depth - 1
        mods = _adaln(cvec, ada_w[l], ada_b[l])
        mod_l = mods[:bsz].reshape(bsz, 6, d)
        mod_c = jnp.broadcast_to(mods[bsz].reshape(1, 6, d), (bsz, 6, d))

        wl = w_in[l]
        w_hg = wl[:, :o_rw].astype(BF16)
        w_rw = rw_reorder(wl[:, o_rw:o_da]).astype(BF16)
        w_da = wl[:, o_da:o_gt].astype(BF16)
        w_gt = wl[:, o_gt:].astype(BF16)
        gmix = norm_mix_g[l].reshape(1, d)
        proj = lambda xx, mod, w, tm: _inproj(xx, gmix, mod, w, tm)

        fh_c, fh_l = _hgrn2(proj(xc, mod_c, w_hg, tm_c), proj(xl, mod_l, w_hg, tm_l),
                            hg_lb_logits, hg_norm_g[l].reshape(1, HG_DK), l)

        mu = rw_reorder(rw_mu[l]).reshape(1, -1)
        consts = (mu, rw_w0[l], zpad(rw_w2[l, 0], LANES),
                  jnp.concatenate([jnp.zeros_like(rw_w2[l, 1]), rw_w2[l, 1]], axis=0),
                  rw_a0[l].reshape(1, RW_C), zpad(rw_a2[l], LANES), rw_g2[l],
                  rw_k_k[l].reshape(1, RW_C), rw_k_a[l].reshape(1, RW_C), hsum)
        prep_c = _rwkv_prep(proj(xc, mod_c, w_rw, tm_c), consts, tm_c)
        prep_l = _rwkv_prep(proj(xl, mod_l, w_rw, tm_l), consts, _row_tile(tl, 256))
        s0 = jnp.zeros((bsz, 2, RW_HD, RW_C), F32)
        yc_f, yc_b, s_ctx = _rwkv_scan(prep_c, s0, _row_tile(tc, 256))
        yl_f, yl_b, _ = _rwkv_scan(prep_l, s_ctx, _row_tile(tl, 256))
        fin = lambda yf, yb, prep, tm: _rwkv_fin(yf, yb, prep, rw_r_k[l].reshape(1, RW_C),
                                                 rw_ln_g[l].reshape(1, RW_C), rw_ln_b[l].reshape(1, RW_C),
                                                 hmean, hsum, tm)
        fr_l = fin(yl_f, yl_b, prep_l, _row_tile(tl, 256))

        pda_c = proj(xc, mod_c, w_da, tm_c)
        pda_l = proj(xl, mod_l, w_da, tm_l)
        fd_l, fd_c = _diff_attn(pda_c, pda_l, cos, sin, da_lambda[l], da_subln_g[l].reshape(1, -1), l, need_ctx,
                                _row_tile(tl, 256))

        projs = (hg_proj[l].astype(BF16), rw_proj[l].astype(BF16), da_proj[l].astype(BF16), w_out[l].astype(BF16))
        xl = _merge(xl, mod_l, fh_l, fr_l, fd_l, proj(xl, mod_l, w_gt, tm_l), *projs, tm_l)
        if need_ctx:
            fr_c = fin(yc_f, yc_b, prep_c, tm_c)
            xc = _merge(xc, mod_c, fh_c, fr_c, fd_c, proj(xc, mod_c, w_gt, tm_c), *projs, tm_c)

        gffn = norm_ffn_g[l].reshape(1, d)
        fg = final_norm_g.reshape(1, d) if last else None
        j = l // 2
        if l % 2 == 0:
            w1, w3, w2 = ffn_w1[j].astype(BF16), ffn_w3[j].astype(BF16), ffn_w2[j].astype(BF16)
            tf = w1.shape[1] // 2
            xl = _ffn(xl, gffn, mod_l, w1, w3, w2, fg, tm_l, tf)
            if need_ctx:
                xc = _ffn(xc, gffn, mod_c, w1, w3, w2, None, tm_c, tf)
        else:
            w1, w3, w2 = moe_w1[j].astype(BF16), moe_w3[j].astype(BF16), moe_w2[j].astype(BF16)
            tf = w1.shape[2] // 2
            router = jnp.pad(moe_router[j], ((0, 0), (0, LANES - N_EXPERTS)))
            tm_e = _row_tile(tl, 1024)
            xl = _moe(xl, gffn, mod_l, router, w1, w3, w2, fg, tm_e, tf, _moe_capacity(tm_e))
            if need_ctx:
                xc = _moe(xc, gffn, mod_c, router, w1, w3, w2, None, tm_c, tf, _moe_capacity(tm_c))
    if depth == 0:
        raise ValueError("depth must be positive")
    return xl
```

```python
import functools
import math

import numpy as np
import jax
import jax.numpy as jnp
from jax import lax
from jax.experimental import pallas as pl
from jax.experimental.pallas import tpu as pltpu

F32 = jnp.float32
BF16 = jnp.bfloat16

GRID_W = 64
HG_HEADS = 4
HG_DK = 128
RW_HEADS = 8
RW_HD = 64
RW_C = RW_HEADS * RW_HD
RW_W_RANK = 64
RW_A_RANK = 64
RW_G_RANK = 128
RW_LN_EPS = 64e-5
DA_HEADS = 4
DA_HD = 64
DA_SUBLN_EPS = 1e-5
DA_SCALE = DA_HD ** -0.5
ROPE_AXIS_DIM = DA_HD // 2
ROPE_BASE = 10000.0
N_EXPERTS = 8
TOP_K = 2
NORM_EPS = 1e-6

LANES = 128
PREP_TILE = 256
RW_CHUNK = 16
RW_BATCH = 1
VMEM_LIMIT = 52 * 1024 * 1024


def _cparams(*sem):
    return pltpu.CompilerParams(dimension_semantics=sem, vmem_limit_bytes=VMEM_LIMIT)


def _dot(a, b):
    return jnp.dot(a, b, preferred_element_type=F32)


def _dot_nt(a, b):
    return lax.dot_general(a, b, (((1,), (1,)), ((), ())), preferred_element_type=F32)


def _dot_tn(a, b):
    return lax.dot_general(a, b, (((0,), (0,)), ((), ())), preferred_element_type=F32)


def _split2(x):
    hi = x.astype(BF16)
    lo = (x - hi.astype(F32)).astype(BF16)
    return hi, lo


def _split3(x):
    hi = x.astype(BF16)
    r = x - hi.astype(F32)
    mid = r.astype(BF16)
    lo = (r - mid.astype(F32)).astype(BF16)
    return hi, mid, lo


def _dot_hi(a, b):
    ah, al = _split2(a)
    bh, bl = _split2(b)
    return _dot(ah, bh) + (_dot(ah, bl) + _dot(al, bh))


def _dot_x3(x, m):
    hi, mid, lo = _split3(x)
    return _dot(hi, m) + (_dot(mid, m) + _dot(lo, m))


def _dot_m3(m, x):
    hi, mid, lo = _split3(x)
    return _dot(m, hi) + (_dot(m, mid) + _dot(m, lo))


def _sigmoid(x):
    return jax.nn.sigmoid(x)


def _silu(x):
    return x * jax.nn.sigmoid(x)


def _log_sigmoid(z):
    return jnp.minimum(z, 0.0) - jnp.log1p(jnp.exp(-jnp.abs(z)))


def _softplus(x):
    return jnp.maximum(x, 0.0) + jnp.log1p(jnp.exp(-jnp.abs(x)))


def _rms(x, eps):
    return x * lax.rsqrt(jnp.mean(x * x, axis=-1, keepdims=True) + eps)


def _adaln_kernel(c_ref, w_ref, b_ref, o_ref):
    o_ref[...] = _dot_hi(_silu(c_ref[...]), w_ref[...]) + b_ref[...]


def _adaln(cvec, w, b):
    rows, d = cvec.shape
    n = w.shape[1]
    tn = n // 4
    return pl.pallas_call(
        _adaln_kernel,
        out_shape=jax.ShapeDtypeStruct((rows, n), F32),
        grid=(n // tn,),
        in_specs=[pl.BlockSpec((rows, d), lambda j: (0, 0)),
                  pl.BlockSpec((d, tn), lambda j: (0, j)),
                  pl.BlockSpec((1, tn), lambda j: (0, j))],
        out_specs=pl.BlockSpec((rows, tn), lambda j: (0, j)),
        compiler_params=_cparams("arbitrary"),
        name="adaln",
    )(cvec, w, b.reshape(1, n))


def _inproj_kernel(x_ref, g_ref, mod_ref, w_ref, o_ref):
    h = _rms(x_ref[...], NORM_EPS) * g_ref[...]
    h = h * (1.0 + mod_ref[1:2, :]) + mod_ref[0:1, :]
    o_ref[...] = _dot(h.astype(BF16), w_ref[...])


def _inproj(x, g, mod, w, tm):
    bsz, t, d = x.shape
    n = w.shape[1]
    return pl.pallas_call(
        _inproj_kernel,
        out_shape=jax.ShapeDtypeStruct((bsz, t, n), F32),
        grid=(bsz, t // tm),
        in_specs=[pl.BlockSpec((None, tm, d), lambda b, i: (b, i, 0)),
                  pl.BlockSpec((1, d), lambda b, i: (0, 0)),
                  pl.BlockSpec((None, 6, d), lambda b, i: (b, 0, 0)),
                  pl.BlockSpec((d, n), lambda b, i: (0, 0))],
        out_specs=pl.BlockSpec((None, tm, n), lambda b, i: (b, i, 0)),
        compiler_params=_cparams("parallel", "parallel"),
        name="inproj",
    )(x, g, mod, w)


def _hgrn2_kernel(layer, tc, tl,
                  qc_ref, ffc_ref, fbc_ref, ic_ref, gc_ref,
                  ql_ref, ffl_ref, fbl_ref, il_ref, gl_ref,
                  lbl_ref, ng_ref, msk_ref,
                  oc_ref, ol_ref,
                  v_s, qe_s, ke_s, et_s, o_s, st_s):
    nct = tc // PREP_TILE
    nlt = tl // PREP_TILE
    pt = PREP_TILE
    levels = [2 ** j for j in range(1, pt.bit_length())]

    lg = lbl_ref[...]
    mx = jnp.max(lg, axis=0, keepdims=True)
    ex = jnp.exp(lg - mx)
    pr = ex / jnp.sum(ex, axis=0, keepdims=True)
    lb = jnp.zeros(lg.shape[1:], F32)
    for j in range(1, layer + 1):
        lb = lb + pr[j]

    ri = lax.broadcasted_iota(jnp.int32, (pt, pt), 0)
    ci = lax.broadcasted_iota(jnp.int32, (pt, pt), 1)
    tri = (jnp.where(ci <= ri, 1.0, 0.0).astype(BF16), jnp.where(ci >= ri, 1.0, 0.0).astype(BF16))
    sub = lax.broadcasted_iota(jnp.int32, (pt // 8, 8, LANES), 1)

    def midpoint(c, g, d):
        r = g // 2 - 1 + d
        if g >= 8:
            c3 = c.reshape(pt // g, g, LANES)
            return jnp.broadcast_to(c3[:, r:r + 1, :], c3.shape).reshape(pt, LANES)
        c3 = c.reshape(pt // 8, 8, LANES)
        m = jnp.broadcast_to(c3[:, r:r + 1, :], c3.shape)
        for j in range(1, 8 // g):
            m = jnp.where(sub >= j * g, jnp.broadcast_to(c3[:, j * g + r:j * g + r + 1, :], c3.shape), m)
        return m.reshape(pt, LANES)

    def prep(q_ref, ff_ref, fb_ref, i_ref, src, dst, tile):
        q = _silu(q_ref[pl.ds(src, pt), :])
        v = i_ref[pl.ds(src, pt), :]
        vb = v.astype(BF16)
        v_s[pl.ds(dst, pt), :] = vb
        for d, z_ref in enumerate((ff_ref, fb_ref)):
            z = z_ref[pl.ds(src, pt), :]
            lbd = lb[d:d + 1, :]
            a = jnp.log(lbd)
            b = jnp.log1p(-lbd) + _log_sigmoid(z)
            lf = jnp.maximum(a, b) + jnp.log1p(jnp.exp(-jnp.abs(a - b)))
            k = (1.0 - lbd) * _sigmoid(-z)
            c = _dot_m3(tri[d], lf)
            tot = c[pt - 1:pt, :] if d == 0 else c[0:1, :]
            att = jnp.zeros((pt, pt), F32)
            for lvl, g in enumerate(levels):
                e = jnp.exp(-jnp.abs(c - midpoint(c, g, d)))
                att = att + _dot_nt((q * e).astype(BF16), (k * e).astype(BF16)) * msk_ref[d, lvl]
            o = _dot(att.astype(BF16), vb) + jnp.sum(q * k, axis=1, keepdims=True) * v
            o_s[d, pl.ds(dst, pt), :] = o
            qe_s[d, pl.ds(dst, pt), :] = (q * jnp.exp(c)).astype(BF16)
            ke_s[d, pl.ds(dst, pt), :] = (k * jnp.exp(tot - c)).astype(BF16)
            et_s[d, pl.ds(pl.multiple_of(tile * 8, 8), 8), :] = jnp.broadcast_to(jnp.exp(tot), (8, LANES))

    for j in range(nct):
        prep(qc_ref, ffc_ref, fbc_ref, ic_ref, j * pt, j * pt, j)

    def prep_lat(j, carry):
        src = pl.multiple_of(j * pt, pt)
        prep(ql_ref, ffl_ref, fbl_ref, il_ref, src, pl.multiple_of(tc + j * pt, pt), nct + j)
        return carry

    lax.fori_loop(0, nlt, prep_lat, 0)

    st_s[...] = jnp.zeros(st_s.shape, F32)

    def carry_state(d, tile):
        sl = pl.ds(pl.multiple_of(tile * pt, pt), pt)
        st = st_s[d]
        o_s[d, sl, :] = o_s[d, sl, :] + _dot_nt(qe_s[d, sl, :], st.astype(BF16))
        upd = _dot_tn(v_s[sl, :], ke_s[d, sl, :])
        st_s[d] = st * et_s[d, pl.ds(pl.multiple_of(tile * 8, 8), 1), :] + upd

    def step(i, carry):
        carry_state(0, i)
        carry_state(1, jnp.where(i < nct, nct - 1 - i, nct + nlt - 1 - (i - nct)))
        return carry

    lax.fori_loop(0, nct + nlt, step, 0)

    ng = ng_ref[...]

    def fin(g_ref, o_ref, src, dst):
        o = o_s[0, pl.ds(dst, pt), :] + o_s[1, pl.ds(dst, pt), :]
        y = _rms(o, NORM_EPS) * ng
        o_ref[pl.ds(src, pt), :] = (y * _silu(g_ref[pl.ds(src, pt), :])).astype(o_ref.dtype)

    for j in range(tc // pt):
        fin(gc_ref, oc_ref, j * pt, j * pt)

    def fin_lat(j, carry):
        fin(gl_ref, ol_ref, pl.multiple_of(j * pt, pt), pl.multiple_of(tc + j * pt, pt))
        return carry

    lax.fori_loop(0, tl // pt, fin_lat, 0)


def _gla_level_masks():
    pt = PREP_TILE
    t = np.arange(pt)[:, None]
    s = np.arange(pt)[None, :]
    out = np.zeros((2, pt.bit_length() - 1, pt, pt), np.float32)
    for lvl in range(out.shape[1]):
        g = 2 << lvl
        same = (t // g) == (s // g)
        out[0, lvl] = same & (t % g >= g // 2) & (s % g < g // 2)
        out[1, lvl] = same & (t % g < g // 2) & (s % g >= g // 2)
    return jnp.asarray(out)


def _hgrn2(p_c, p_l, lb_logits, norm_g, layer):
    bsz, tc, _ = p_c.shape
    tl = p_l.shape[1]
    t = tc + tl
    depth = lb_logits.shape[0]
    nh = HG_HEADS
    w = HG_DK

    def col(tx, j):
        return pl.BlockSpec((None, tx, w), lambda b, h, j=j: (b, 0, j * nh + h))

    masks = _gla_level_masks()
    in_specs = ([col(tc, j) for j in range(5)] + [col(tl, j) for j in range(5)]
                + [pl.BlockSpec((depth, 2, w), lambda b, h: (0, 0, h)),
                   pl.BlockSpec((1, w), lambda b, h: (0, 0)),
                   pl.BlockSpec(masks.shape, lambda b, h: (0, 0, 0, 0))])
    out_specs = [pl.BlockSpec((None, tc, w), lambda b, h: (b, 0, h)),
                 pl.BlockSpec((None, tl, w), lambda b, h: (b, 0, h))]
    scratch = [pltpu.VMEM((t, w), BF16),
               pltpu.VMEM((2, t, w), BF16), pltpu.VMEM((2, t, w), BF16),
               pltpu.VMEM((2, 8 * (t // PREP_TILE), w), F32), pltpu.VMEM((2, t, w), F32),
               pltpu.VMEM((2, w, w), F32)]
    return pl.pallas_call(
        functools.partial(_hgrn2_kernel, layer, tc, tl),
        out_shape=[jax.ShapeDtypeStruct((bsz, tc, nh * w), BF16),
                   jax.ShapeDtypeStruct((bsz, tl, nh * w), BF16)],
        grid=(bsz, nh),
        in_specs=in_specs,
        out_specs=out_specs,
        scratch_shapes=scratch,
        compiler_params=_cparams("parallel", "parallel"),
        name="hgrn2",
    )(p_c, p_c, p_c, p_c, p_c, p_l, p_l, p_l, p_l, p_l, lb_logits, norm_g, masks)


def _rwkv_prep_kernel(nt, p_ref, pp_ref, pn_ref, mu_ref, w0_ref, w2f_ref, w2b_ref, a0_ref, a2_ref, g2_ref,
                      kk_ref, ka_ref, hsum_ref,
                      r_ref, lwf_ref, lwb_ref, k_ref, an_ref, bb_ref, v_ref, g_ref):
    i = pl.program_id(1)
    p = p_ref[...]
    tm = p.shape[0]
    rows = lax.broadcasted_iota(jnp.int32, p.shape, 0)
    first = jnp.where(i > 0, pp_ref[7:8, :], 0.0)
    last = jnp.where(i < nt - 1, pn_ref[0:1, :], 0.0)
    prev = jnp.where(rows == 0, first, pltpu.roll(p, 1, 0))
    nxt = jnp.where(rows == tm - 1, last, pltpu.roll(p, tm - 1, 0))
    xs = p + mu_ref[...] * (0.5 * (prev + nxt) - p)

    c = RW_C
    r = xs[:, 0:c]
    k = xs[:, c:2 * c]
    v = xs[:, 2 * c:3 * c]
    wd = jnp.tanh(xs[:, 3 * c:3 * c + LANES])
    gd = _sigmoid(xs[:, 3 * c + LANES:3 * c + 2 * LANES])
    ad = xs[:, 3 * c + 2 * LANES:3 * c + 3 * LANES]

    a = _sigmoid(a0_ref[...] + _dot_hi(ad, a2_ref[...]))
    kk = k * kk_ref[...]
    nrm = jnp.sqrt(_dot_x3(kk * kk, hsum_ref[...]))
    kk = kk / jnp.maximum(nrm, 1e-12)
    kp = k * (1.0 + (a - 1.0) * ka_ref[...])
    g = _dot_hi(gd, g2_ref[...])

    def log_decay(w2_ref, d):
        wlog = -_softplus(-(w0_ref[d:d + 1, :] + _dot_hi(wd, w2_ref[...]))) - 0.5
        return -jnp.exp(wlog)

    r_ref[...] = r
    lwf_ref[...] = log_decay(w2f_ref, 0)
    lwb_ref[...] = log_decay(w2b_ref, 1)
    k_ref[...] = kp
    an_ref[...] = -kk
    bb_ref[...] = kk * a
    v_ref[...] = v
    g_ref[...] = g


def _rwkv_prep(p, consts, tm):
    bsz, t, n = p.shape
    nt = t // tm
    c = RW_C
    hb = tm // 8
    full = lambda a: pl.BlockSpec(a.shape, lambda b, i, nd=a.ndim: (0,) * nd)
    in_specs = ([pl.BlockSpec((None, tm, n), lambda b, i: (b, i, 0)),
                 pl.BlockSpec((None, 8, n), lambda b, i: (b, jnp.maximum(i * hb - 1, 0), 0)),
                 pl.BlockSpec((None, 8, n), lambda b, i: (b, jnp.minimum((i + 1) * hb, t // 8 - 1), 0))]
                + [full(a) for a in consts])
    out_specs = [pl.BlockSpec((None, tm, c), lambda b, i: (b, i, 0))] * 8
    out_shape = [jax.ShapeDtypeStruct((bsz, t, c), F32)] * 8
    return pl.pallas_call(
        functools.partial(_rwkv_prep_kernel, nt),
        out_shape=out_shape,
        grid=(bsz, nt),
        in_specs=in_specs,
        out_specs=out_specs,
        compiler_params=_cparams("parallel", "parallel"),
        name="rwkv_prep",
    )(p, p, p, *consts)


def _rwkv_scan_kernel(*refs):
    dir_in = (refs[0:6], refs[6:12])
    s0_ref = refs[12]
    y_refs = refs[13:15]
    sout_ref = refs[15]
    s_scr, x6_s = refs[16:]
    nh, hd, c = RW_HEADS, RW_HD, RW_CHUNK
    n = c * nh
    i = pl.program_id(1)
    nbat = dir_in[0][0].shape[0]
    nch = dir_in[0][0].shape[1] // c

    @pl.when(i == 0)
    def _():
        s_scr[...] = s0_ref[...]

    tb = nch * c
    nl = nh * hd // LANES
    ti = lax.broadcasted_iota(jnp.int32, (tb, tb), 0)
    tj = lax.broadcasted_iota(jnp.int32, (tb, tb), 1)
    same = (ti // c) == (tj // c)
    m_all = jnp.where(same, 1.0, 0.0).astype(BF16)
    tri = (jnp.where(same & (tj <= ti), 1.0, 0.0).astype(BF16), jnp.where(same & (tj >= ti), 1.0, 0.0).astype(BF16))

    for q in range(nbat):
        for d in range(2):
            r_ref, lw_ref, k_ref, a_ref, b_ref, _ = dir_in[d]
            lw = lw_ref[q]
            g = _dot_m3(tri[d], lw)
            tot = _dot_m3(m_all, lw)
            ineg = jnp.exp(-g)
            etg = jnp.exp(tot - g)
            a, b, k = a_ref[q], b_ref[q], k_ref[q]
            groups = (a * jnp.exp(g - lw), r_ref[q] * jnp.exp(g), b * ineg, k * ineg, b * etg, k * etg, jnp.exp(tot))
            for j, val in enumerate(groups):
                for m in range(nl):
                    x6_s[q, d, j, m] = val[:, m * LANES:(m + 1) * LANES]

    tr = lax.broadcasted_iota(jnp.int32, (n, n), 0) // nh
    ts = lax.broadcasted_iota(jnp.int32, (n, n), 1) // nh
    strict = (ts < tr, ts > tr)
    incl = (ts <= tr, ts >= tr)
    rc = lax.broadcasted_iota(jnp.int32, (n, c), 0) // nh
    cc = lax.broadcasted_iota(jnp.int32, (n, c), 1)
    strict_c = (cc < rc, cc > rc)
    hrow = lax.broadcasted_iota(jnp.int32, (nh, nh * hd), 0)
    hcol = lax.broadcasted_iota(jnp.int32, (nh, nh * hd), 1) // hd
    own = jnp.where(hrow == hcol, 1.0, 0.0)

    def chunk(q, d, c0):
        v_ref = dir_in[d][5]
        sls = pl.ds(pl.multiple_of(c0 * nh, n), n)

        def rows(j, start, size):
            return jnp.concatenate([x6_s[q, d, j, m, pl.ds(start, size), :] for m in range(nl)], axis=1)

        xx = jnp.concatenate([rows(j, c0 + t, 1) * own for j in range(6) for t in range(c)], axis=0).astype(BF16)
        lx, rx, bx = xx[0:2 * n], xx[2 * n:4 * n], xx[4 * n:6 * n]
        gm = _dot_nt(lx, rx)
        aab = jnp.where(strict_c[d], _dot_nt(lx[0:n], rows(2, c0, c).astype(BF16)), 0.0)
        aak = jnp.where(strict[d], gm[0:n, n:2 * n], 0.0)
        arb = jnp.where(incl[d], gm[n:2 * n, 0:n], 0.0)
        ark = jnp.where(incl[d], gm[n:2 * n, n:2 * n], 0.0)
        vb = v_ref[q, sls, :].astype(BF16)
        st = s_scr[q, d]
        w = _dot_nt(lx, st.astype(BF16))
        rhs = w[0:n] + _dot(aak.astype(BF16), vb)
        blocks = [rhs[nh * t:nh * (t + 1)] for t in range(c)]
        order = range(c) if d == 0 else range(c - 1, -1, -1)
        for s in order:
            later = range(s + 1, c) if d == 0 else range(0, s)
            for t in later:
                blocks[t] = blocks[t] + aab[nh * t:nh * (t + 1), s:s + 1] * blocks[s]
        u = jnp.concatenate(blocks, axis=0)
        uv = jnp.concatenate([u.astype(BF16), vb], axis=0)
        y_refs[d][q, sls, :] = w[n:2 * n] + _dot(jnp.concatenate([arb, ark], axis=1).astype(BF16), uv)
        s_scr[q, d] = st * rows(6, c0, 1) + _dot_tn(uv, bx)

    def step(j, carry):
        for q in range(nbat):
            chunk(q, 0, pl.multiple_of(j * c, c))
            chunk(q, 1, pl.multiple_of((nch - 1 - j) * c, c))
        return carry

    lax.fori_loop(0, nch, step, 0)

    @pl.when(i == pl.num_programs(1) - 1)
    def _():
        sout_ref[...] = s_scr[...]


def _rwkv_scan(prep, s0, tb):
    r, lwf, lwb, k, an, bb, v = prep[:7]
    bsz, t, ch = r.shape
    nh, hd = RW_HEADS, RW_HD
    nblk = t // tb
    vs = v.reshape(bsz, t * nh, hd)
    nb = RW_BATCH if bsz % RW_BATCH == 0 else 1
    fwd = lambda b, i: (b, i, 0)
    bwd = lambda b, i: (b, nblk - 1 - i, 0)
    nat = lambda im: pl.BlockSpec((nb, tb, ch), im)
    stk = lambda im: pl.BlockSpec((nb, tb * nh, hd), im)
    sspec = pl.BlockSpec((nb, 2, hd, ch), lambda b, i: (b, 0, 0, 0))
    yshape = jax.ShapeDtypeStruct((bsz, t * nh, hd), F32)
    yf, yb, s_out = pl.pallas_call(
        _rwkv_scan_kernel,
        out_shape=[yshape, yshape, jax.ShapeDtypeStruct(s0.shape, F32)],
        grid=(bsz // nb, nblk),
        in_specs=[nat(fwd)] * 5 + [stk(fwd)] + [nat(bwd)] * 5 + [stk(bwd)] + [sspec],
        out_specs=[stk(fwd), stk(bwd), sspec],
        scratch_shapes=[pltpu.VMEM((nb, 2, hd, ch), F32), pltpu.VMEM((nb, 2, 7, ch // LANES, tb, LANES), F32)],
        compiler_params=_cparams("parallel", "arbitrary"),
        name="rwkv_scan",
    )(r, lwf, k, an, bb, vs, r, lwb, k, an, bb, vs, s0)
    return yf.reshape(bsz, t, ch), yb.reshape(bsz, t, ch), s_out


def _rwkv_fin_kernel(yf_ref, yb_ref, r_ref, k_ref, v_ref, g_ref, rk_ref, lng_ref, lnb_ref, hmean_ref, hsum_ref, o_ref):
    y = yf_ref[...] + yb_ref[...]
    r, k, v = r_ref[...], k_ref[...], v_ref[...]
    mean = _dot_x3(y, hmean_ref[...])
    yc = y - mean
    var = _dot_x3(yc * yc, hmean_ref[...])
    yn = yc * lax.rsqrt(var + RW_LN_EPS) * lng_ref[...] + lnb_ref[...]
    bonus = _dot_x3(r * k * rk_ref[...], hsum_ref[...]) * v
    o_ref[...] = ((yn + bonus) * g_ref[...]).astype(o_ref.dtype)


def _rwkv_fin(yf, yb, prep, rk, lng, lnb, hmean, hsum, tm):
    bsz, t, c = yf.shape
    r, _, _, k, _, _, v, g = prep
    tspec = pl.BlockSpec((None, tm, c), lambda b, i: (b, i, 0))
    vec = pl.BlockSpec((1, c), lambda b, i: (0, 0))
    mat = pl.BlockSpec((c, c), lambda b, i: (0, 0))
    return pl.pallas_call(
        _rwkv_fin_kernel,
        out_shape=jax.ShapeDtypeStruct((bsz, t, c), BF16),
        grid=(bsz, t // tm),
        in_specs=[tspec] * 6 + [vec, vec, vec, mat, mat],
        out_specs=tspec,
        compiler_params=_cparams("parallel", "parallel"),
        name="rwkv_fin",
    )(yf, yb, r, k, v, g, rk, lng, lnb, hmean, hsum)


def _rope(x, cos, sin):
    lane = lax.broadcasted_iota(jnp.int32, x.shape, 1)
    partner = jnp.where((lane % ROPE_AXIS_DIM) < ROPE_AXIS_DIM // 2,
                        pltpu.roll(x, LANES - ROPE_AXIS_DIM // 2, 1), pltpu.roll(x, ROPE_AXIS_DIM // 2, 1))
    return x * cos + partner * sin


def _lambda(lam_ref, lam_init):
    lp = lam_ref[...]
    s1 = jnp.sum(lp[0:1, :] * lp[1:2, :], axis=1, keepdims=True)
    s2 = jnp.sum(lp[2:3, :] * lp[3:4, :], axis=1, keepdims=True)
    return jnp.exp(s1) - jnp.exp(s2) + lam_init


def _diff_attend(q, k, v, lam, sg, lam_init):
    lane = lax.broadcasted_iota(jnp.int32, q.shape, 1)
    qb = q.astype(BF16)
    zero = jnp.zeros_like(qb)
    probs = []
    for m in range(2):
        qm = jnp.where((lane // DA_HD) == m, qb, zero)
        s = _dot_nt(qm, k)
        e = jnp.exp(s - jnp.max(s, axis=-1, keepdims=True))
        probs.append(e / jnp.sum(e, axis=-1, keepdims=True))
    w = probs[0] - lam * probs[1]
    o = _dot(w.astype(BF16), v)
    return _rms(o, DA_SUBLN_EPS) * sg * (1.0 - lam_init)


def _da_lat_kernel(lam_init, tc, tl, q_ref, kc_ref, vc_ref, kl_ref, vl_ref, cos_ref, sin_ref, lam_ref, sg_ref,
                   o_ref, k_s, v_s):
    qi = pl.program_id(2)
    tq = q_ref.shape[0]

    @pl.when(qi == 0)
    def _():
        k_s[0:tc, :] = kc_ref[...].astype(BF16)
        v_s[0:tc, :] = vc_ref[...].astype(BF16)
        k_s[tc:tc + tl, :] = _rope(kl_ref[...], cos_ref[...], sin_ref[...]).astype(BF16)
        v_s[tc:tc + tl, :] = vl_ref[...].astype(BF16)

    r0 = pl.multiple_of(qi * tq, tq)
    q = _rope(q_ref[...], cos_ref[pl.ds(r0, tq), :], sin_ref[pl.ds(r0, tq), :]) * DA_SCALE
    o = _diff_attend(q, k_s[...], v_s[...], _lambda(lam_ref, lam_init), sg_ref[...], lam_init)
    o_ref[...] = o.astype(o_ref.dtype)


def _da_ctx_kernel(lam_init, q_ref, k_ref, v_ref, lam_ref, sg_ref, o_ref):
    o = _diff_attend(q_ref[...] * DA_SCALE, k_ref[...].astype(BF16), v_ref[...].astype(BF16),
                     _lambda(lam_ref, lam_init), sg_ref[...], lam_init)
    o_ref[...] = o.astype(o_ref.dtype)


def _diff_attn(p_c, p_l, cos, sin, lam_p, sg, layer, need_ctx, tq):
    bsz, tc, _ = p_c.shape
    tl = p_l.shape[1]
    nh = DA_HEADS
    w = 2 * DA_HD
    lam_init = 0.8 - 0.6 * math.exp(-0.3 * layer)
    small = [pl.BlockSpec(lam_p.shape, lambda *a: (0, 0)), pl.BlockSpec((1, w), lambda *a: (0, 0))]
    o_l = pl.pallas_call(
        functools.partial(_da_lat_kernel, lam_init, tc, tl),
        out_shape=jax.ShapeDtypeStruct((bsz, tl, nh * w), BF16),
        grid=(bsz, nh, tl // tq),
        in_specs=[pl.BlockSpec((None, tq, w), lambda b, h, i: (b, i, h)),
                  pl.BlockSpec((None, tc, w), lambda b, h, i: (b, 0, nh + h)),
                  pl.BlockSpec((None, tc, w), lambda b, h, i: (b, 0, 2 * nh + h)),
                  pl.BlockSpec((None, tl, w), lambda b, h, i: (b, 0, nh + h)),
                  pl.BlockSpec((None, tl, w), lambda b, h, i: (b, 0, 2 * nh + h)),
                  pl.BlockSpec((tl, w), lambda b, h, i: (0, 0)),
                  pl.BlockSpec((tl, w), lambda b, h, i: (0, 0))] + small,
        out_specs=pl.BlockSpec((None, tq, w), lambda b, h, i: (b, i, h)),
        scratch_shapes=[pltpu.VMEM((tc + tl, w), BF16), pltpu.VMEM((tc + tl, w), BF16)],
        compiler_params=_cparams("parallel", "parallel", "arbitrary"),
        name="diff_attn_lat",
    )(p_l, p_c, p_c, p_l, p_l, cos, sin, lam_p, sg)
    o_c = None
    if need_ctx:
        o_c = pl.pallas_call(
            functools.partial(_da_ctx_kernel, lam_init),
            out_shape=jax.ShapeDtypeStruct((bsz, tc, nh * w), BF16),
            grid=(bsz, nh),
            in_specs=[pl.BlockSpec((None, tc, w), lambda b, h: (b, 0, h)),
                      pl.BlockSpec((None, tc, w), lambda b, h: (b, 0, nh + h)),
                      pl.BlockSpec((None, tc, w), lambda b, h: (b, 0, 2 * nh + h))] + small,
            out_specs=pl.BlockSpec((None, tc, w), lambda b, h: (b, 0, h)),
            compiler_params=_cparams("parallel", "parallel"),
            name="diff_attn_ctx",
        )(p_c, p_c, p_c, lam_p, sg)
    return o_l, o_c


def _rope_tables(tl):
    rows = tl // GRID_W
    t = np.arange(tl)
    pos = np.stack([t // GRID_W, t % GRID_W], axis=1).astype(np.float32)
    inv_freq = (1.0 / (ROPE_BASE ** (jnp.arange(0, ROPE_AXIS_DIM, 2, dtype=F32) / ROPE_AXIS_DIM)))
    lane = np.arange(LANES)
    d = lane % DA_HD
    axis = d // ROPE_AXIS_DIM
    freq = d % (ROPE_AXIS_DIM // 2)
    sign = np.where((d % ROPE_AXIS_DIM) < ROPE_AXIS_DIM // 2, -1.0, 1.0).astype(np.float32)
    ang = jnp.asarray(pos)[:, axis] * inv_freq[freq][None, :]
    del rows
    return jnp.cos(ang), jnp.sin(ang) * sign[None, :]


def _merge_kernel(x_ref, mod_ref, fh_ref, fr_ref, fd_ref, gt_ref, ph_ref, pr_ref, pd_ref, wo_ref, o_ref):
    d = x_ref.shape[-1]
    g = gt_ref[...]
    m = (_sigmoid(g[:, 0:d]) * _dot(fh_ref[...], ph_ref[...])
         + _sigmoid(g[:, d:2 * d]) * _dot(fr_ref[...], pr_ref[...])
         + _sigmoid(g[:, 2 * d:3 * d]) * _dot(fd_ref[...], pd_ref[...]))
    o_ref[...] = x_ref[...] + mod_ref[2:3, :] * _dot(m.astype(BF16), wo_ref[...])


def _merge(x, mod, fh, fr, fd, gate, ph, pr, pd, wo, tm):
    bsz, t, d = x.shape
    c = fh.shape[-1]
    row = lambda n: pl.BlockSpec((None, tm, n), lambda b, i: (b, i, 0))
    wspec = lambda a: pl.BlockSpec(a.shape, lambda b, i: (0, 0))
    return pl.pallas_call(
        _merge_kernel,
        out_shape=jax.ShapeDtypeStruct(x.shape, F32),
        grid=(bsz, t // tm),
        in_specs=[row(d), pl.BlockSpec((None, 6, d), lambda b, i: (b, 0, 0)), row(c), row(c), row(c), row(3 * d),
                  wspec(ph), wspec(pr), wspec(pd), wspec(wo)],
        out_specs=row(d),
        input_output_aliases={0: 0},
        compiler_params=_cparams("parallel", "parallel"),
        name="merge",
    )(x, mod, fh, fr, fd, gate, ph, pr, pd, wo)


def _ffn_prologue(x_ref, g_ref, mod_ref):
    h = _rms(x_ref[...], NORM_EPS) * g_ref[...]
    return h * (1.0 + mod_ref[4:5, :]) + mod_ref[3:4, :]


def _ffn_epilogue(x_ref, mod_ref, acc, fg_ref, o_ref):
    y = x_ref[...] + mod_ref[5:6, :] * acc
    if fg_ref is not None:
        y = _rms(y, NORM_EPS) * fg_ref[...]
    o_ref[...] = y


def _swiglu_step(h, w1_ref, w3_ref, w2_ref):
    a = _dot(h, w1_ref[...])
    u = _silu(a) * _dot(h, w3_ref[...])
    return _dot(u.astype(BF16), w2_ref[...])


def _ffn_kernel(final, x_ref, g_ref, mod_ref, w1_ref, w3_ref, w2_ref, *rest):
    fg_ref, (o_ref, h_s, acc_s) = (rest[0], rest[1:]) if final else (None, rest)
    f = pl.program_id(2)

    @pl.when(f == 0)
    def _():
        h_s[...] = _ffn_prologue(x_ref, g_ref, mod_ref).astype(BF16)
        acc_s[...] = jnp.zeros(acc_s.shape, F32)

    acc_s[...] += _swiglu_step(h_s[...], w1_ref, w3_ref, w2_ref)

    @pl.when(f == pl.num_programs(2) - 1)
    def _():
        _ffn_epilogue(x_ref, mod_ref, acc_s[...], fg_ref, o_ref)


def _ffn(x, g, mod, w1, w3, w2, final_g, tm, tf):
    bsz, t, d = x.shape
    fdim = w1.shape[1]
    final = final_g is not None
    in_specs = [pl.BlockSpec((None, tm, d), lambda b, i, f: (b, i, 0)),
                pl.BlockSpec((1, d), lambda b, i, f: (0, 0)),
                pl.BlockSpec((None, 6, d), lambda b, i, f: (b, 0, 0)),
                pl.BlockSpec((d, tf), lambda b, i, f: (0, f)),
                pl.BlockSpec((d, tf), lambda b, i, f: (0, f)),
                pl.BlockSpec((tf, d), lambda b, i, f: (f, 0))]
    args = [x, g, mod, w1, w3, w2]
    if final:
        in_specs.append(pl.BlockSpec((1, d), lambda b, i, f: (0, 0)))
        args.append(final_g)
    return pl.pallas_call(
        functools.partial(_ffn_kernel, final),
        out_shape=jax.ShapeDtypeStruct(x.shape, F32),
        grid=(bsz, t // tm, fdim // tf),
        in_specs=in_specs,
        out_specs=pl.BlockSpec((None, tm, d), lambda b, i, f: (b, i, 0)),
        scratch_shapes=[pltpu.VMEM((tm, d), BF16), pltpu.VMEM((tm, d), F32)],
        compiler_params=_cparams("parallel", "parallel", "arbitrary"),
        name="ffn",
    )(*args)


def _moe_route_kernel(x_ref, g_ref, mod_ref, rt_ref, tri_ref, h_ref, cmb_ref, pos_ref):
    h = _ffn_prologue(x_ref, g_ref, mod_ref)
    h_ref[...] = h.astype(BF16)
    lane = lax.broadcasted_iota(jnp.int32, cmb_ref.shape, 1).astype(F32)
    logits = jnp.where(lane < N_EXPERTS, _dot_hi(h, rt_ref[...]), -jnp.inf)
    m1 = jnp.max(logits, axis=-1, keepdims=True)
    i1 = jnp.min(jnp.where(logits == m1, lane, float(LANES)), axis=-1, keepdims=True)
    rest_l = jnp.where(lane == i1, -jnp.inf, logits)
    m2 = jnp.max(rest_l, axis=-1, keepdims=True)
    i2 = jnp.min(jnp.where(rest_l == m2, lane, float(LANES)), axis=-1, keepdims=True)
    e2 = jnp.exp(m2 - m1)
    cmb = jnp.where(lane == i1, 1.0 / (1.0 + e2), 0.0) + jnp.where(lane == i2, e2 / (1.0 + e2), 0.0)
    cmb_ref[...] = cmb
    pos_ref[...] = _dot(tri_ref[...], jnp.where(cmb > 0.0, 1.0, 0.0).astype(BF16))


def _moe_expert_kernel(cap, h_ref, cmb_ref, pos_ref, w1_ref, w3_ref, w2_ref, o_ref, acc_s, sel_s, he_s, y_s):
    e = pl.program_id(2)
    f = pl.program_id(3)
    last_f = pl.num_programs(3) - 1
    tm = cmb_ref.shape[0]
    lane = lax.broadcasted_iota(jnp.int32, cmb_ref.shape, 1)
    ce = jnp.sum(jnp.where(lane == e, cmb_ref[...], 0.0), axis=-1, keepdims=True)
    pe = jnp.sum(jnp.where(lane == e, pos_ref[...], 0.0), axis=-1, keepdims=True)
    slot = lax.broadcasted_iota(jnp.int32, (tm, cap), 1).astype(F32)

    def onehot(first):
        return jnp.where((pe - first == slot) & (ce > 0.0), 1.0, 0.0).astype(BF16)

    @pl.when((e == 0) & (f == 0))
    def _():
        acc_s[...] = jnp.zeros(acc_s.shape, F32)

    @pl.when(f == 0)
    def _():
        sel = onehot(0.0)
        sel_s[...] = sel
        he_s[...] = _dot_tn(sel, h_ref[...]).astype(BF16)
        y_s[...] = jnp.zeros(y_s.shape, F32)

    y_s[...] += _swiglu_step(he_s[...], w1_ref, w3_ref, w2_ref)

    @pl.when(f == last_f)
    def _():
        acc_s[...] += ce * _dot(sel_s[...], y_s[...].astype(BF16))

    count = jnp.sum(jnp.where(ce > 0.0, 1.0, 0.0))

    def overflow(j, carry):
        sel = onehot((j * cap).astype(F32))
        hej = _dot_tn(sel, h_ref[...]).astype(BF16)
        yj = _swiglu_step(hej, w1_ref, w3_ref, w2_ref)
        acc_s[...] += ce * _dot(sel, yj.astype(BF16))
        return carry

    lax.fori_loop(1, jnp.ceil(count / cap).astype(jnp.int32), overflow, 0)

    @pl.when((e == pl.num_programs(2) - 1) & (f == last_f))
    def _():
        o_ref[...] = acc_s[...]


def _moe_finish_kernel(final, x_ref, mod_ref, y_ref, *rest):
    fg_ref, o_ref = rest if final else (None, rest[0])
    _ffn_epilogue(x_ref, mod_ref, y_ref[...], fg_ref, o_ref)


def _moe(x, g, mod, router, w1, w3, w2, final_g, tm, tf, cap):
    bsz, t, d = x.shape
    ne, _, fdim = w1.shape
    final = final_g is not None
    tri = jnp.asarray(np.tril(np.ones((tm, tm), np.float32), -1), BF16)
    row = lambda n: pl.BlockSpec((None, tm, n), lambda b, i: (b, i, 0))
    h, cmb, pos = pl.pallas_call(
        _moe_route_kernel,
        out_shape=[jax.ShapeDtypeStruct((bsz, t, d), BF16), jax.ShapeDtypeStruct((bsz, t, LANES), F32),
                   jax.ShapeDtypeStruct((bsz, t, LANES), F32)],
        grid=(bsz, t // tm),
        in_specs=[row(d), pl.BlockSpec((1, d), lambda b, i: (0, 0)), pl.BlockSpec((None, 6, d), lambda b, i: (b, 0, 0)),
                  pl.BlockSpec((d, LANES), lambda b, i: (0, 0)), pl.BlockSpec((tm, tm), lambda b, i: (0, 0))],
        out_specs=[row(d), row(LANES), row(LANES)],
        compiler_params=_cparams("parallel", "parallel"),
        name="moe_route",
    )(x, g, mod, router, tri)
    row4 = lambda n: pl.BlockSpec((None, tm, n), lambda b, i, e, f: (b, i, 0))
    y = pl.pallas_call(
        functools.partial(_moe_expert_kernel, cap),
        out_shape=jax.ShapeDtypeStruct((bsz, t, d), F32),
        grid=(bsz, t // tm, ne, fdim // tf),
        in_specs=[row4(d), row4(LANES), row4(LANES),
                  pl.BlockSpec((None, d, tf), lambda b, i, e, f: (e, 0, f)),
                  pl.BlockSpec((None, d, tf), lambda b, i, e, f: (e, 0, f)),
                  pl.BlockSpec((None, tf, d), lambda b, i, e, f: (e, f, 0))],
        out_specs=row4(d),
        scratch_shapes=[pltpu.VMEM((tm, d), F32), pltpu.VMEM((tm, cap), BF16), pltpu.VMEM((cap, d), BF16),
                        pltpu.VMEM((cap, d), F32)],
        compiler_params=_cparams("parallel", "parallel", "arbitrary", "arbitrary"),
        name="moe_experts",
    )(h, cmb, pos, w1, w3, w2)
    tmf = _row_tile(t, 512)
    rowf = lambda n: pl.BlockSpec((None, tmf, n), lambda b, i: (b, i, 0))
    in_specs = [rowf(d), pl.BlockSpec((None, 6, d), lambda b, i: (b, 0, 0)), rowf(d)]
    args = [x, mod, y]
    if final:
        in_specs.append(pl.BlockSpec((1, d), lambda b, i: (0, 0)))
        args.append(final_g)
    return pl.pallas_call(
        functools.partial(_moe_finish_kernel, final),
        out_shape=jax.ShapeDtypeStruct(x.shape, F32),
        grid=(bsz, t // tmf),
        in_specs=in_specs,
        out_specs=rowf(d),
        compiler_params=_cparams("parallel", "parallel"),
        name="moe_finish",
    )(*args)


def _moe_capacity(tm):
    even = tm * TOP_K // N_EXPERTS
    return -(-(even + even // 8) // 16) * 16


def _row_tile(t, pref):
    tm = min(pref, t)
    while t % tm:
        tm //= 2
    return tm


def kernel(x, c, ctx, c_ctx, ada_w, ada_b, norm_mix_g, norm_ffn_g, final_norm_g, w_in, hg_lb_logits, hg_norm_g,
           hg_proj, rw_mu, rw_w0, rw_w2, rw_a0, rw_a2, rw_g2, rw_k_k, rw_k_a, rw_r_k, rw_ln_g, rw_ln_b, rw_proj,
           da_lambda, da_subln_g, da_proj, w_out, ffn_w1, ffn_w3, ffn_w2, moe_router, moe_w1, moe_w3, moe_w2):
    bsz, tl, d = x.shape
    tc = ctx.shape[1]
    depth = ada_w.shape[0]
    hg_cols = 5 * HG_HEADS * HG_DK
    rw_cols = 3 * RW_C + 2 * RW_W_RANK + RW_A_RANK + RW_G_RANK
    da_cols = 3 * DA_HEADS * 2 * DA_HD
    o_rw = hg_cols
    o_da = o_rw + rw_cols
    o_gt = o_da + da_cols

    ad_lo = 3 * RW_C + 2 * RW_W_RANK
    ad_hi = ad_lo + RW_A_RANK

    def rw_reorder(a):
        pad = jnp.zeros(a.shape[:-1] + ((-rw_cols) % LANES,), a.dtype)
        return jnp.concatenate([a[..., :ad_lo], a[..., ad_hi:], a[..., ad_lo:ad_hi], pad], axis=-1)

    cvec = jnp.zeros((16, d), F32).at[:bsz].set(c).at[bsz].set(c_ctx)
    cos, sin = _rope_tables(tl)
    head_id = np.arange(RW_C) // RW_HD
    hsum = jnp.asarray((head_id[:, None] == head_id[None, :]).astype(np.float32), BF16)
    hmean = (hsum.astype(F32) / RW_HD).astype(BF16)
    zpad = lambda a, rows: jnp.concatenate([a, jnp.zeros((rows - a.shape[0],) + a.shape[1:], a.dtype)], axis=0)

    tm_l = _row_tile(tl, 512)
    tm_c = _row_tile(tc, 256)
    xl, xc = x, ctx
    for l in range(depth):
        need_ctx = l < depth - 1
        last = l == depth - 1
        mods = _adaln(cvec, ada_w[l], ada_b[l])
        mod_l = mods[:bsz].reshape(bsz, 6, d)
        mod_c = jnp.broadcast_to(mods[bsz].reshape(1, 6, d), (bsz, 6, d))

        wl = w_in[l]
        w_hg = wl[:, :o_rw].astype(BF16)
        w_rw = rw_reorder(wl[:, o_rw:o_da]).astype(BF16)
        w_da = wl[:, o_da:o_gt].astype(BF16)
        w_gt = wl[:, o_gt:].astype(BF16)
        gmix = norm_mix_g[l].reshape(1, d)
        proj = lambda xx, mod, w, tm: _inproj(xx, gmix, mod, w, tm)

        fh_c, fh_l = _hgrn2(proj(xc, mod_c, w_hg, tm_c), proj(xl, mod_l, w_hg, tm_l),
                            hg_lb_logits, hg_norm_g[l].reshape(1, HG_DK), l)

        mu = rw_reorder(rw_mu[l]).reshape(1, -1)
        consts = (mu, rw_w0[l], zpad(rw_w2[l, 0], LANES),
                  jnp.concatenate([jnp.zeros_like(rw_w2[l, 1]), rw_w2[l, 1]], axis=0),
                  rw_a0[l].reshape(1, RW_C), zpad(rw_a2[l], LANES), rw_g2[l],
                  rw_k_k[l].reshape(1, RW_C), rw_k_a[l].reshape(1, RW_C), hsum)
        prep_c = _rwkv_prep(proj(xc, mod_c, w_rw, tm_c), consts, tm_c)
        prep_l = _rwkv_prep(proj(xl, mod_l, w_rw, tm_l), consts, _row_tile(tl, 256))
        s0 = jnp.zeros((bsz, 2, RW_HD, RW_C), F32)
        yc_f, yc_b, s_ctx = _rwkv_scan(prep_c, s0, _row_tile(tc, 256))
        yl_f, yl_b, _ = _rwkv_scan(prep_l, s_ctx, _row_tile(tl, 256))
        fin = lambda yf, yb, prep, tm: _rwkv_fin(yf, yb, prep, rw_r_k[l].reshape(1, RW_C),
                                                 rw_ln_g[l].reshape(1, RW_C), rw_ln_b[l].reshape(1, RW_C),
                                                 hmean, hsum, tm)
        fr_l = fin(yl_f, yl_b, prep_l, _row_tile(tl, 256))

        pda_c = proj(xc, mod_c, w_da, tm_c)
        pda_l = proj(xl, mod_l, w_da, tm_l)
        fd_l, fd_c = _diff_attn(pda_c, pda_l, cos, sin, da_lambda[l], da_subln_g[l].reshape(1, -1), l, need_ctx,
                                _row_tile(tl, 256))

        projs = (hg_proj[l].astype(BF16), rw_proj[l].astype(BF16), da_proj[l].astype(BF16), w_out[l].astype(BF16))
        xl = _merge(xl, mod_l, fh_l, fr_l, fd_l, proj(xl, mod_l, w_gt, tm_l), *projs, tm_l)
        if need_ctx:
            fr_c = fin(yc_f, yc_b, prep_c, tm_c)
            xc = _merge(xc, mod_c, fh_c, fr_c, fd_c, proj(xc, mod_c, w_gt, tm_c), *projs, tm_c)

        gffn = norm_ffn_g[l].reshape(1, d)
        fg = final_norm_g.reshape(1, d) if last else None
        j = l // 2
        if l % 2 == 0:
            w1, w3, w2 = ffn_w1[j].astype(BF16), ffn_w3[j].astype(BF16), ffn_w2[j].astype(BF16)
            tf = w1.shape[1] // 2
            xl = _ffn(xl, gffn, mod_l, w1, w3, w2, fg, tm_l, tf)
            if need_ctx:
                xc = _ffn(xc, gffn, mod_c, w1, w3, w2, None, tm_c, tf)
        else:
            w1, w3, w2 = moe_w1[j].astype(BF16), moe_w3[j].astype(BF16), moe_w2[j].astype(BF16)
            tf = w1.shape[2] // 2
            router = jnp.pad(moe_router[j], ((0, 0), (0, LANES - N_EXPERTS)))
            tm_e = _row_tile(tl, 1024)
            xl = _moe(xl, gffn, mod_l, router, w1, w3, w2, fg, tm_e, tf, _moe_capacity(tm_e))
            if need_ctx:
                xc = _moe(xc, gffn, mod_c, router, w1, w3, w2, None, tm_c, tf, _moe_capacity(tm_c))
    if depth == 0:
        raise ValueError("depth must be positive")
    return xl
```
